```python
import math, functools
import jax, jax.numpy as jnp
from jax import lax
import numpy as np

D_MODEL = 2048
BATCH = 8
SEQ = 2048
DEPTH = 1
DEC_BATCH = 32
DEC_SEQ = 4
PAST_LEN = 8192
PAGE_SIZE = 128

N_HEADS = 16
HEAD_DIM = D_MODEL // N_HEADS
ATT_WIDTH = N_HEADS * HEAD_DIM
MOBA_BLOCK = 256
MOBA_TOPK = 3
QUERY_BLOCK = 8
SSM_EXPAND = 2
D_INNER = SSM_EXPAND * D_MODEL
SSM_HEADDIM = 64
SSM_HEADS = D_INNER // SSM_HEADDIM
SSM_GROUPS = 8
SSM_STATE = 128
CONV_W = 4
CONV_DIM = D_INNER + 2 * SSM_GROUPS * SSM_STATE
SSD_CHUNK = 128
N_EXPERT_GROUPS = 4
EXPERTS_PER_GROUP = 8
N_EXPERTS = N_EXPERT_GROUPS * EXPERTS_PER_GROUP
TOPK_IN_GROUP = 2
EXPERT_HIDDEN = D_MODEL // 4
MOE_BLOCK = 128
IN_COLS = 3 * ATT_WIDTH + D_INNER + CONV_DIM + SSM_HEADS + 2 * D_MODEL
NORM_EPS = 1e-6

kernel_name = 'moba_ssd_hier_moe_step'


def rms_norm(x, g):
    xf = x.astype(jnp.float32)
    xf = xf * lax.rsqrt(jnp.mean(xf * xf, axis=-1, keepdims=True) + NORM_EPS)
    return xf.astype(x.dtype) * g


def alibi_slopes():
    return 2.0 ** (-8.0 * jnp.arange(1, N_HEADS + 1, dtype=jnp.float32) / N_HEADS)


def moba_attend(q, t_pos, own_k, own_v, own_pos, slopes, sel):
    f32 = jnp.float32
    scale = q.shape[-1] ** -0.5
    m = slopes[:, None, None]
    tq = t_pos.astype(f32)[:, None]
    s_own = jnp.einsum('bhqd,bhld->bhql', q, own_k, preferred_element_type=f32) * scale \
        - m * (tq - own_pos.astype(f32)[None, :])
    s_own = jnp.where(own_pos[None, :] <= t_pos[:, None], s_own, -jnp.inf)
    if sel is None:
        p = jax.nn.softmax(s_own, axis=-1).astype(own_v.dtype)
        return jnp.einsum('bhql,bhld->bhqd', p, own_v)
    sel_k, sel_v, sel_pos, sel_valid = sel
    s_sel = jnp.einsum('bhqd,bhqsd->bhqs', q, sel_k, preferred_element_type=f32) * scale \
        - m * (tq - sel_pos.astype(f32))
    if sel_valid is not None:
        s_sel = jnp.where(sel_valid, s_sel, -jnp.inf)
    n_sel = s_sel.shape[-1]
    p = jax.nn.softmax(jnp.concatenate([s_sel, s_own], axis=-1), axis=-1).astype(own_v.dtype)
    return jnp.einsum('bhqs,bhqsd->bhqd', p[..., :n_sel], sel_v) \
        + jnp.einsum('bhql,bhld->bhqd', p[..., n_sel:], own_v)


def moba_prompt(q, k, v, slopes):
    f32 = jnp.float32
    nb, t, nh, hd = q.shape
    n_blk = -(-t // MOBA_BLOCK)
    pad = n_blk * MOBA_BLOCK - t
    kp = jnp.pad(k, ((0, 0), (0, pad), (0, 0), (0, 0))).reshape(nb, n_blk, MOBA_BLOCK, nh, hd)
    vp = jnp.pad(v, ((0, 0), (0, pad), (0, 0), (0, 0))).reshape(nb, n_blk, MOBA_BLOCK, nh, hd)
    k_mean = jnp.mean(kp, axis=2, dtype=f32)
    kbh = kp.transpose(0, 3, 1, 2, 4)
    vbh = vp.transpose(0, 3, 1, 2, 4)
    qh = q.transpose(0, 2, 1, 3)
    t_pos = jnp.arange(t)
    n_past = t_pos // MOBA_BLOCK
    gate = jnp.einsum('bhtd,bjhd->bhtj', qh.astype(f32), k_mean)
    gate = jnp.where(jnp.arange(n_blk)[None, :] < n_past[:, None], gate, -jnp.inf)
    k_eff = min(MOBA_TOPK, n_blk)
    _, idx = lax.top_k(gate, k_eff)
    valid = jnp.repeat(jnp.arange(k_eff)[None, :] < n_past[:, None], MOBA_BLOCK, axis=1)
    bi = jnp.arange(nb)[:, None, None, None]
    hi = jnp.arange(nh)[None, :, None, None]
    blk_off = jnp.arange(MOBA_BLOCK)

    def attend_query_block(n):
        t0 = n * QUERY_BLOCK
        qb = lax.dynamic_slice_in_dim(qh, t0, QUERY_BLOCK, axis=2)
        ib = lax.dynamic_slice_in_dim(idx, t0, QUERY_BLOCK, axis=2)
        vb = lax.dynamic_slice_in_dim(valid, t0, QUERY_BLOCK, axis=0)
        tq = t0 + jnp.arange(QUERY_BLOCK)
        sel_k = kbh[bi, hi, ib].reshape(nb, nh, QUERY_BLOCK, -1, hd)
        sel_v = vbh[bi, hi, ib].reshape(nb, nh, QUERY_BLOCK, -1, hd)
        sel_pos = (ib[..., None] * MOBA_BLOCK + blk_off).reshape(nb, nh, QUERY_BLOCK, -1)
        ob = t0 // MOBA_BLOCK
        own_k = lax.dynamic_index_in_dim(kbh, ob, axis=2, keepdims=False)
        own_v = lax.dynamic_index_in_dim(vbh, ob, axis=2, keepdims=False)
        own_pos = ob * MOBA_BLOCK + blk_off
        return moba_attend(qb, tq, own_k, own_v, own_pos, slopes, (sel_k, sel_v, sel_pos, vb))

    out = lax.map(attend_query_block, jnp.arange(t // QUERY_BLOCK))
    return out.transpose(1, 0, 3, 2, 4).reshape(nb, t, nh, hd)


def moba_sample(q, k, v, cache_k, cache_v, page_table, slopes):
    f32 = jnp.float32
    nb, tq, nh, hd = q.shape
    n_pages = page_table.shape[1]
    past = n_pages * PAGE_SIZE
    ppb = MOBA_BLOCK // PAGE_SIZE
    t_pos = past + jnp.arange(tq)
    n_full = past // MOBA_BLOCK
    qh = q.transpose(0, 2, 1, 3)
    sel = None
    if n_full > 0:
        k_full = cache_k[page_table[:, :n_full * ppb]].reshape(nb, n_full, MOBA_BLOCK, nh, hd)
        k_mean = jnp.mean(k_full, axis=2, dtype=f32)
        gate = jnp.einsum('bhqd,bjhd->bhqj', qh.astype(f32), k_mean)
        _, idx = lax.top_k(gate, min(MOBA_TOPK, n_full))
        logical = idx[..., None] * ppb + jnp.arange(ppb)
        phys = page_table[jnp.arange(nb)[:, None, None, None, None], logical]
        hi = jnp.arange(nh)[None, :, None, None, None]
        sel_k = cache_k[phys, :, hi].reshape(nb, nh, tq, -1, hd).astype(q.dtype)
        sel_v = cache_v[phys, :, hi].reshape(nb, nh, tq, -1, hd).astype(q.dtype)
        sel_pos = (idx[..., None] * MOBA_BLOCK + jnp.arange(MOBA_BLOCK)).reshape(nb, nh, tq, -1)
        sel = (sel_k, sel_v, sel_pos, None)
    own_start = n_full * MOBA_BLOCK
    own_cached = past - own_start
    own_k = k.transpose(0, 2, 1, 3)
    own_v = v.transpose(0, 2, 1, 3)
    if own_cached > 0:
        pages = page_table[:, own_start // PAGE_SIZE:]
        k_old = cache_k[pages].reshape(nb, own_cached, nh, hd).transpose(0, 2, 1, 3).astype(q.dtype)
        v_old = cache_v[pages].reshape(nb, own_cached, nh, hd).transpose(0, 2, 1, 3).astype(q.dtype)
        own_k = jnp.concatenate([k_old, own_k], axis=2)
        own_v = jnp.concatenate([v_old, own_v], axis=2)
    own_pos = own_start + jnp.arange(own_cached + tq)
    out = moba_attend(qh, t_pos, own_k, own_v, own_pos, slopes, sel)
    return out.transpose(0, 2, 1, 3)


def causal_depthwise_conv(xpad, w):
    return lax.conv_general_dilated(xpad, w[:, None, :].astype(xpad.dtype), window_strides=(1,),
                                    padding='VALID', dimension_numbers=('NWC', 'WIO', 'NWC'),
                                    feature_group_count=xpad.shape[-1])


def ssd_scan(x, dt, a, bm, cm, init_state):
    f32 = jnp.float32
    nb, t, nh, p = x.shape
    g, n = bm.shape[2], bm.shape[3]
    j = nh // g
    l = min(SSD_CHUNK, t)
    c = t // l
    dt = dt.reshape(nb, c, l, g, j)
    xdt = x.astype(f32).reshape(nb, c, l, g, j, p) * dt[..., None]
    a_cs = jnp.cumsum((a.reshape(g, j) * dt).transpose(0, 3, 4, 1, 2), axis=-1)
    bf = bm.astype(f32).reshape(nb, c, l, g, n)
    cf = cm.astype(f32).reshape(nb, c, l, g, n)
    causal = jnp.tril(jnp.ones((l, l), dtype=bool))
    decay_in = jnp.exp(jnp.where(causal, a_cs[..., :, None] - a_cs[..., None, :], -jnp.inf))
    cb = jnp.einsum('bclgn,bcsgn->bgcls', cf, bf)
    y_diag = jnp.einsum('bgjcls,bcsgjp->bclgjp', cb[:, :, None] * decay_in, xdt)
    decay_to_end = jnp.exp(a_cs[..., -1:] - a_cs)
    chunk_states = jnp.einsum('bclgn,bgjcl,bclgjp->bcgjpn', bf, decay_to_end, xdt)
    chunk_decay = jnp.exp(a_cs[..., -1])

    def carry_state(s, inp):
        st, dec = inp
        return dec[..., None, None] * s + st, s

    s0 = init_state.astype(f32).reshape(nb, g, j, p, n)
    s_final, s_prev = lax.scan(carry_state, s0, (chunk_states.transpose(1, 0, 2, 3, 4, 5),
                                                 chunk_decay.transpose(3, 0, 1, 2)))
    y_off = jnp.einsum('bclgn,cbgjpn,bgjcl->bclgjp', cf, s_prev, jnp.exp(a_cs))
    return (y_diag + y_off).reshape(nb, t, nh, p), s_final.reshape(nb, nh, p, n)


def hier_moe(h, w_group, b_group, w_router, b_router, w_up, w_gate_e, w_down):
    f32 = jnp.float32
    shape = h.shape
    x = h.reshape(-1, shape[-1])
    t = x.shape[0]
    g_logits = jnp.matmul(x, w_group, preferred_element_type=f32) + b_group.astype(f32)
    g_top = jnp.argmax(g_logits, axis=-1)
    g_weight = jnp.take_along_axis(jax.nn.softmax(g_logits, axis=-1), g_top[:, None], axis=-1)
    e_logits = (jnp.matmul(x, w_router, preferred_element_type=f32) + b_router.astype(f32)).reshape(
        t, N_EXPERT_GROUPS, EXPERTS_PER_GROUP)
    e_in_group = jnp.take_along_axis(e_logits, g_top[:, None, None], axis=1)[:, 0]
    e_val, e_loc = lax.top_k(e_in_group, TOPK_IN_GROUP)
    e_weight = jax.nn.softmax(e_val, axis=-1) * g_weight
    e_id = g_top[:, None] * EXPERTS_PER_GROUP + e_loc
    n_assign = t * TOPK_IN_GROUP
    flat_e = e_id.reshape(n_assign)
    flat_t = jnp.repeat(jnp.arange(t, dtype=jnp.int32), TOPK_IN_GROUP)
    flat_w = e_weight.reshape(n_assign)
    order = jnp.argsort(flat_e)
    sorted_e = flat_e[order]
    counts = jnp.bincount(flat_e, length=N_EXPERTS)
    padded = (counts + MOE_BLOCK - 1) // MOE_BLOCK * MOE_BLOCK
    pad_end = jnp.cumsum(padded)
    dest = (pad_end - padded)[sorted_e] + jnp.arange(n_assign) - (jnp.cumsum(counts) - counts)[sorted_e]
    n_blocks = (n_assign + N_EXPERTS * (MOE_BLOCK - 1) + MOE_BLOCK - 1) // MOE_BLOCK
    n_slots = n_blocks * MOE_BLOCK
    slot_tok = jnp.full((n_slots,), t, jnp.int32).at[dest].set(flat_t[order])
    slot_w = jnp.zeros((n_slots,), f32).at[dest].set(flat_w[order])
    block_expert = jnp.minimum(jnp.searchsorted(pad_end, jnp.arange(n_blocks) * MOE_BLOCK, side='right'),
                               N_EXPERTS - 1)
    xs = jnp.concatenate([x, jnp.zeros((1, x.shape[-1]), x.dtype)], axis=0)[slot_tok]
    xs = xs.reshape(n_blocks, MOE_BLOCK, -1)

    def expert_block(args):
        xb, e = args
        return (jax.nn.silu(xb @ w_gate_e[e]) * (xb @ w_up[e])) @ w_down[e]

    yb = lax.map(expert_block, (xs, block_expert)).reshape(n_slots, -1)
    y = jax.ops.segment_sum(yb * slot_w[:, None].astype(yb.dtype), slot_tok, num_segments=t + 1)[:t]
    return y.reshape(shape)


def hybrid_layer(x, c, attn_fn, ssm_init, conv_init, w_ada, b_ada, g_mix, w_in, b_gate, conv_w, conv_b,
                 dt_bias, a_log, d_skip, g_ssm, w_attn_br, w_ssm_br, w_out, g_ffn, w_group, b_group,
                 w_router, b_router, w_up, w_gate_e, w_down):
    f32 = jnp.float32
    nb, t, _ = x.shape
    mod = (jax.nn.silu(c) @ w_ada + b_ada)[:, None, :]
    shift1, scale1, gate1, shift2, scale2, gate2 = jnp.split(mod, 6, axis=-1)
    h = rms_norm(x, g_mix) * (1.0 + scale1) + shift1
    cuts = np.cumsum([ATT_WIDTH, ATT_WIDTH, ATT_WIDTH, D_INNER, CONV_DIM, SSM_HEADS, D_MODEL]).tolist()
    q, k, v, z, xbc, dt_raw, ga, gs = jnp.split(h @ w_in, cuts, axis=-1)
    q = q.reshape(nb, t, N_HEADS, HEAD_DIM)
    k = k.reshape(nb, t, N_HEADS, HEAD_DIM)
    v = v.reshape(nb, t, N_HEADS, HEAD_DIM)
    attn = attn_fn(q, k, v).reshape(nb, t, ATT_WIDTH)
    xpad = jnp.concatenate([conv_init.astype(xbc.dtype), xbc], axis=1)
    xbc_c = jax.nn.silu(causal_depthwise_conv(xpad, conv_w) + conv_b)
    xs, bm, cm = jnp.split(xbc_c, [D_INNER, D_INNER + SSM_GROUPS * SSM_STATE], axis=-1)
    xs = xs.reshape(nb, t, SSM_HEADS, SSM_HEADDIM)
    dt = jax.nn.softplus(dt_raw.astype(f32) + dt_bias.astype(f32))
    a = -jnp.exp(a_log.astype(f32))
    y, ssm_new = ssd_scan(xs, dt, a, bm.reshape(nb, t, SSM_GROUPS, SSM_STATE),
                          cm.reshape(nb, t, SSM_GROUPS, SSM_STATE), ssm_init)
    y = (y + d_skip.astype(f32)[:, None] * xs.astype(f32)).reshape(nb, t, D_INNER).astype(x.dtype)
    y = rms_norm(y * jax.nn.silu(z), g_ssm)
    merged = jax.nn.sigmoid(ga + b_gate[:D_MODEL]) * (attn @ w_attn_br) \
        + jax.nn.sigmoid(gs + b_gate[D_MODEL:]) * (y @ w_ssm_br)
    x = x + gate1 * (merged @ w_out)
    h2 = rms_norm(x, g_ffn) * (1.0 + scale2) + shift2
    x = x + gate2 * hier_moe(h2, w_group, b_group, w_router, b_router, w_up, w_gate_e, w_down)
    return x, k, v, ssm_new.astype(ssm_init.dtype), xpad[:, -(CONV_W - 1):]


def setup_inputs(seed: int = 0) -> dict:
    key = jax.random.key(seed)
    ks = jax.random.split(key, 32)
    f32 = jnp.float32

    def nrm(i, shape, scale):
        return scale * jax.random.normal(ks[i], shape, f32)

    n_pages = PAST_LEN // PAGE_SIZE
    n_phys = (5 * DEC_BATCH * n_pages + 3) // 4
    page_table = jax.random.permutation(ks[6], n_phys)[:DEC_BATCH * n_pages].reshape(
        DEC_BATCH, n_pages).astype(jnp.int32)
    dt0 = jnp.exp(jax.random.uniform(ks[15], (DEPTH, SSM_HEADS), f32, math.log(1e-3), math.log(1e-1)))
    dt_bias = dt0 + jnp.log(-jnp.expm1(-dt0))
    a_log = jnp.log(jax.random.uniform(ks[16], (DEPTH, SSM_HEADS), f32, 1.0, 16.0))
    return {
        'x_prompt': nrm(0, (BATCH, SEQ, D_MODEL), 1.0),
        'x_sample': nrm(1, (DEC_BATCH, DEC_SEQ, D_MODEL), 1.0),
        'cache_k': nrm(2, (DEPTH, n_phys, PAGE_SIZE, N_HEADS, HEAD_DIM), 1.0),
        'cache_v': nrm(3, (DEPTH, n_phys, PAGE_SIZE, N_HEADS, HEAD_DIM), 1.0),
        'state_ssm': nrm(4, (DEPTH, DEC_BATCH, SSM_HEADS, SSM_HEADDIM, SSM_STATE), 0.5),
        'state_conv': nrm(5, (DEPTH, DEC_BATCH, CONV_W - 1, CONV_DIM), 1.0),
        'page_table': page_table,
        'c_prompt': nrm(7, (BATCH, D_MODEL), 1.0),
        'c_sample': nrm(8, (DEC_BATCH, D_MODEL), 1.0),
        'w_ada': nrm(9, (DEPTH, D_MODEL, 6 * D_MODEL), 0.5 * D_MODEL ** -0.5),
        'b_ada': nrm(10, (DEPTH, 6 * D_MODEL), 0.02),
        'g_mix': 1.0 + nrm(11, (DEPTH, D_MODEL), 0.02),
        'w_in': nrm(12, (DEPTH, D_MODEL, IN_COLS), D_MODEL ** -0.5),
        'b_gate': nrm(13, (DEPTH, 2 * D_MODEL), 0.02),
        'conv_w': nrm(14, (DEPTH, CONV_W, CONV_DIM), CONV_W ** -0.5),
        'conv_b': nrm(17, (DEPTH, CONV_DIM), 0.02),
        'dt_bias': dt_bias,
        'a_log': a_log,
        'd_skip': 1.0 + nrm(18, (DEPTH, SSM_HEADS), 0.02),
        'g_ssm': 1.0 + nrm(19, (DEPTH, D_INNER), 0.02),
        'w_attn_br': nrm(20, (DEPTH, ATT_WIDTH, D_MODEL), ATT_WIDTH ** -0.5),
        'w_ssm_br': nrm(21, (DEPTH, D_INNER, D_MODEL), D_INNER ** -0.5),
        'w_out': nrm(22, (DEPTH, D_MODEL, D_MODEL), D_MODEL ** -0.5),
        'g_ffn': 1.0 + nrm(23, (DEPTH, D_MODEL), 0.02),
        'w_group': nrm(24, (DEPTH, D_MODEL, N_EXPERT_GROUPS), D_MODEL ** -0.5),
        'b_group': nrm(25, (DEPTH, N_EXPERT_GROUPS), 0.01),
        'w_router': nrm(26, (DEPTH, D_MODEL, N_EXPERTS), D_MODEL ** -0.5),
        'b_router': nrm(27, (DEPTH, N_EXPERTS), 0.01),
        'w_up': nrm(28, (DEPTH, N_EXPERTS, D_MODEL, EXPERT_HIDDEN), D_MODEL ** -0.5),
        'w_gate_e': nrm(29, (DEPTH, N_EXPERTS, D_MODEL, EXPERT_HIDDEN), D_MODEL ** -0.5),
        'w_down': nrm(30, (DEPTH, N_EXPERTS, EXPERT_HIDDEN, D_MODEL), EXPERT_HIDDEN ** -0.5),
        'g_final': 1.0 + nrm(31, (D_MODEL,), 0.02),
    }


def reference(x_prompt, x_sample, cache_k, cache_v, state_ssm, state_conv, page_table, c_prompt, c_sample,
              w_ada, b_ada, g_mix, w_in, b_gate, conv_w, conv_b, dt_bias, a_log, d_skip, g_ssm,
              w_attn_br, w_ssm_br, w_out, g_ffn, w_group, b_group, w_router, b_router,
              w_up, w_gate_e, w_down, g_final):
    slopes = alibi_slopes()
    bp = x_prompt.shape[0]
    hp, hs = x_prompt, x_sample
    k_p, v_p, k_s, v_s, ssm_p, ssm_s, conv_p, conv_s = [], [], [], [], [], [], [], []
    for l in range(DEPTH):
        lw = [w[l] for w in (w_ada, b_ada, g_mix, w_in, b_gate, conv_w, conv_b, dt_bias, a_log, d_skip,
                             g_ssm, w_attn_br, w_ssm_br, w_out, g_ffn, w_group, b_group, w_router,
                             b_router, w_up, w_gate_e, w_down)]
        prompt_attn = functools.partial(moba_prompt, slopes=slopes)
        sample_attn = functools.partial(moba_sample, cache_k=cache_k[l], cache_v=cache_v[l],
                                        page_table=page_table, slopes=slopes)
        ssm0 = jnp.zeros((bp, SSM_HEADS, SSM_HEADDIM, SSM_STATE), x_prompt.dtype)
        conv0 = jnp.zeros((bp, CONV_W - 1, CONV_DIM), x_prompt.dtype)
        hp, kl, vl, sl, cl = hybrid_layer(hp, c_prompt, prompt_attn, ssm0, conv0, *lw)
        k_p.append(kl); v_p.append(vl); ssm_p.append(sl); conv_p.append(cl)
        hs, kl, vl, sl, cl = hybrid_layer(hs, c_sample, sample_attn, state_ssm[l], state_conv[l], *lw)
        k_s.append(kl); v_s.append(vl); ssm_s.append(sl); conv_s.append(cl)
    y_prompt = rms_norm(hp, g_final)
    y_sample = rms_norm(hs, g_final)
    return (y_prompt, y_sample, jnp.stack(k_p), jnp.stack(v_p), jnp.stack(k_s), jnp.stack(v_s),
            jnp.stack(ssm_p), jnp.stack(ssm_s), jnp.stack(conv_p), jnp.stack(conv_s))
```

```python
import functools

import jax
import jax.numpy as jnp
from jax import lax
from jax.experimental import pallas as pl
from jax.experimental.pallas import tpu as pltpu

F32 = jnp.float32
BF16 = jnp.bfloat16
NEG_INF = float("-inf")

PAGE_SIZE = 128
N_HEADS = 16
MOBA_BLOCK = 256
MOBA_TOPK = 3
SSM_HEADDIM = 64
SSM_GROUPS = 8
SSM_STATE = 128
HEADS_PER_GROUP = 8
SSD_CHUNK = 128
N_EXPERT_GROUPS = 4
EXPERTS_PER_GROUP = 8
N_EXPERTS = N_EXPERT_GROUPS * EXPERTS_PER_GROUP
MOE_BLOCK = 128
NORM_EPS = 1e-6

LANES = 128
VMEM_LIMIT_MB = 56


def _params(semantics, vmem_mb=VMEM_LIMIT_MB):
    return pltpu.CompilerParams(dimension_semantics=semantics,
                                vmem_limit_bytes=vmem_mb * 1024 * 1024)


def _sigmoid(x):
    return 1.0 / (1.0 + jnp.exp(-x))


def _silu(x):
    return x * _sigmoid(x)


def _softplus(x):
    return jnp.maximum(x, 0.0) + jnp.log1p(jnp.exp(-jnp.abs(x)))


def _dot(a, b):
    return jnp.dot(a, b, preferred_element_type=F32)


def _dot_nt(a, b):
    return lax.dot_general(a, b, (((1,), (1,)), ((), ())), preferred_element_type=F32)


def _split2(x):
    hi = x.astype(BF16)
    lo = (x - hi.astype(F32)).astype(BF16)
    return hi, lo


def _split3(x):
    p1 = x.astype(BF16)
    r1 = x - p1.astype(F32)
    p2 = r1.astype(BF16)
    p3 = (r1 - p2.astype(F32)).astype(BF16)
    return p1, p2, p3


def _dot_exact_rhs(x, e):
    p1, p2, p3 = _split3(x)
    return _dot(p1, e) + _dot(p2, e) + _dot(p3, e)


def _dot_exact_lhs(e, x):
    p1, p2, p3 = _split3(x)
    return _dot(e, p1) + _dot(e, p2) + _dot(e, p3)


def _dot_hi_nt(a, b):
    ah, al = _split2(a)
    bh, bl = _split2(b)
    return _dot_nt(ah, bh) + _dot_nt(ah, bl) + _dot_nt(al, bh)


def _ada_kernel(c_ref, w_ref, b_ref, o_ref):
    a = _silu(c_ref[...]).astype(BF16)
    o_ref[...] = _dot(a, w_ref[...].astype(BF16)) + b_ref[...]


def _ada_mod(c, w_ada, b_ada):
    mc, d = c.shape
    n = w_ada.shape[1]
    tn = 1024
    return pl.pallas_call(
        _ada_kernel,
        grid=(n // tn,),
        in_specs=[pl.BlockSpec((mc, d), lambda j: (0, 0)),
                  pl.BlockSpec((d, tn), lambda j: (0, j)),
                  pl.BlockSpec((1, tn), lambda j: (0, j))],
        out_specs=pl.BlockSpec((mc, tn), lambda j: (0, j)),
        out_shape=jax.ShapeDtypeStruct((mc, n), F32),
        compiler_params=_params(("parallel",)),
        name="ada_mod",
    )(c, w_ada, b_ada.reshape(1, n))


def _modnorm_kernel(x_ref, g_ref, sc_ref, sh_ref, o_ref):
    x = x_ref[0]
    xn = x * lax.rsqrt(jnp.mean(x * x, axis=-1, keepdims=True) + NORM_EPS)
    h = (xn * g_ref[...]) * (1.0 + sc_ref[0, 0]) + sh_ref[0, 0]
    o_ref[...] = h.astype(o_ref.dtype)


def _modnorm(x3, g, mod4, scale_idx, shift_idx, tt):
    bm, t, d = x3.shape
    r = mod4.shape[2]
    assert r == 1 or (r == t and tt == t)
    nt = t // tt
    return pl.pallas_call(
        _modnorm_kernel,
        grid=(bm, nt),
        in_specs=[pl.BlockSpec((1, tt, d), lambda b, i: (b, i, 0)),
                  pl.BlockSpec((1, d), lambda b, i: (0, 0)),
                  pl.BlockSpec((1, 1, r, d), lambda b, i: (scale_idx, b, 0, 0)),
                  pl.BlockSpec((1, 1, r, d), lambda b, i: (shift_idx, b, 0, 0))],
        out_specs=pl.BlockSpec((tt, d), lambda b, i: (b * nt + i, 0)),
        out_shape=jax.ShapeDtypeStruct((bm * t, d), BF16),
        compiler_params=_params(("parallel", "parallel")),
        name="modnorm",
    )(x3, g.reshape(1, d), mod4, mod4)


def _mm_kernel(a_ref, w_ref, o_ref):
    o_ref[...] = _dot(a_ref[...], w_ref[...]).astype(o_ref.dtype)


def _matmul(a, w, tm, tn, out_dtype=F32):
    m, k = a.shape
    n = w.shape[1]
    tm = min(tm, m)
    tn = min(tn, n)
    return pl.pallas_call(
        _mm_kernel,
        grid=(m // tm, n // tn),
        in_specs=[pl.BlockSpec((tm, k), lambda i, j: (i, 0)),
                  pl.BlockSpec((k, tn), lambda i, j: (0, j))],
        out_specs=pl.BlockSpec((tm, tn), lambda i, j: (i, j)),
        out_shape=jax.ShapeDtypeStruct((m, n), out_dtype),
        compiler_params=_params(("parallel", "parallel")),
        name="matmul",
    )(a, w)


def _moba_prompt_kernel(slopes_ref, q_ref, k_ref, v_ref, o_ref,
                        kb_ref, vb_ref, km_ref, m_ref, l_ref, acc_ref, *, n_blk, scale):
    h = pl.program_id(1)
    ob = pl.program_id(2)
    blk = MOBA_BLOCK

    @pl.when(ob == 0)
    def _():
        k = k_ref[0]
        kb_ref[...] = k.astype(BF16)
        vb_ref[...] = v_ref[0].astype(BF16)
        for j in range(n_blk):
            km_ref[j:j + 1, :] = jnp.sum(k[j * blk:(j + 1) * blk], axis=0, keepdims=True) * (1.0 / blk)

    slope = slopes_ref[h]
    q = q_ref[0]
    qb = q.astype(BF16)
    gate = _dot_hi_nt(q, km_ref[...])
    jidx = lax.broadcasted_iota(jnp.int32, gate.shape, 1)
    gate = jnp.where(jidx < ob, gate, NEG_INF)
    row = lax.broadcasted_iota(jnp.int32, (blk, blk), 0)
    col = lax.broadcasted_iota(jnp.int32, (blk, blk), 1)
    rel = (row - col).astype(F32)

    start = pl.multiple_of(ob * blk, blk)
    s = _dot_nt(qb, kb_ref[pl.ds(start, blk), :]) * scale - slope * rel
    s = jnp.where(col <= row, s, NEG_INF)
    m0 = jnp.max(s, axis=1, keepdims=True)
    p = jnp.exp(s - m0)
    m_ref[...] = m0
    l_ref[...] = jnp.sum(p, axis=1, keepdims=True)
    acc_ref[...] = _dot(p.astype(BF16), vb_ref[pl.ds(start, blk), :])

    for j in range(n_blk - 1):
        @pl.when(j < ob)
        def _(j=j):
            gj = gate[:, j:j + 1]
            beats = (gate > gj) | ((gate == gj) & (jidx < j))
            rank = jnp.sum(jnp.where(beats, 1.0, 0.0), axis=1, keepdims=True)
            sel = rank < MOBA_TOPK
            dist = rel + ((ob - j) * blk).astype(F32)
            sj = _dot_nt(qb, kb_ref[j * blk:(j + 1) * blk, :]) * scale - slope * dist
            sj = jnp.where(sel, sj, NEG_INF)
            m_old = m_ref[...]
            m_new = jnp.maximum(m_old, jnp.max(sj, axis=1, keepdims=True))
            alpha = jnp.exp(m_old - m_new)
            pj = jnp.exp(sj - m_new)
            l_ref[...] = alpha * l_ref[...] + jnp.sum(pj, axis=1, keepdims=True)
            acc_ref[...] = alpha * acc_ref[...] + _dot(pj.astype(BF16), vb_ref[j * blk:(j + 1) * blk, :])
            m_ref[...] = m_new

    o_ref[0] = (acc_ref[...] / l_ref[...]).astype(o_ref.dtype)


def _moba_prompt(q3, k3, v3, slopes):
    b, t, w = q3.shape
    hd = w // N_HEADS
    assert t % MOBA_BLOCK == 0 and hd == LANES
    n_blk = t // MOBA_BLOCK
    assert n_blk >= MOBA_TOPK
    kern = functools.partial(_moba_prompt_kernel, n_blk=n_blk, scale=hd ** -0.5)
    grid_spec = pltpu.PrefetchScalarGridSpec(
        num_scalar_prefetch=1,
        grid=(b, N_HEADS, n_blk),
        in_specs=[pl.BlockSpec((1, MOBA_BLOCK, hd), lambda bi, h, i, s: (bi, i, h)),
                  pl.BlockSpec((1, t, hd), lambda bi, h, i, s: (bi, 0, h)),
                  pl.BlockSpec((1, t, hd), lambda bi, h, i, s: (bi, 0, h))],
        out_specs=pl.BlockSpec((1, MOBA_BLOCK, hd), lambda bi, h, i, s: (bi, i, h)),
        scratch_shapes=[pltpu.VMEM((t, hd), BF16), pltpu.VMEM((t, hd), BF16),
                        pltpu.VMEM((n_blk, hd), F32),
                        pltpu.VMEM((MOBA_BLOCK, 1), F32), pltpu.VMEM((MOBA_BLOCK, 1), F32),
                        pltpu.VMEM((MOBA_BLOCK, hd), F32)])
    return pl.pallas_call(
        kern, grid_spec=grid_spec,
        out_shape=jax.ShapeDtypeStruct((b, t, w), BF16),
        compiler_params=_params(("parallel", "parallel", "arbitrary")),
        name="moba_prompt",
    )(slopes, q3, k3, v3)


def _kmean_kernel(pt_ref, p0_ref, p1_ref, o_ref):
    s = jnp.sum(p0_ref[0], axis=0, keepdims=True) + jnp.sum(p1_ref[0], axis=0, keepdims=True)
    o_ref[0] = s * (1.0 / MOBA_BLOCK)


def _cache_block_means(cache_k3, page_table):
    bd, n_pages = page_table.shape
    w = cache_k3.shape[2]
    ppb = MOBA_BLOCK // PAGE_SIZE
    assert ppb == 2
    n_full = n_pages // ppb
    grid_spec = pltpu.PrefetchScalarGridSpec(
        num_scalar_prefetch=1,
        grid=(bd, n_full),
        in_specs=[pl.BlockSpec((1, PAGE_SIZE, w), lambda b, j, pt: (pt[b * n_pages + 2 * j], 0, 0)),
                  pl.BlockSpec((1, PAGE_SIZE, w), lambda b, j, pt: (pt[b * n_pages + 2 * j + 1], 0, 0))],
        out_specs=pl.BlockSpec((1, 1, w), lambda b, j, pt: (b * n_full + j, 0, 0)))
    out = pl.pallas_call(
        _kmean_kernel, grid_spec=grid_spec,
        out_shape=jax.ShapeDtypeStruct((bd * n_full, 1, w), F32),
        compiler_params=_params(("parallel", "parallel")),
        name="cache_block_means",
    )(page_table.reshape(-1), cache_k3, cache_k3)
    return out.reshape(bd, n_full, w)


def _sample_topk_kernel(q_ref, km_ref, o_ref, *, n_full):
    q = q_ref[0]
    km = km_ref[0]
    rows = q.shape[0]
    lane = lax.broadcasted_iota(jnp.int32, (rows, n_full), 1).astype(F32)
    lane_o = lax.broadcasted_iota(jnp.int32, (rows, LANES), 1)
    for h in range(N_HEADS):
        sl = slice(h * LANES, (h + 1) * LANES)
        g = _dot_hi_nt(q[:, sl], km[:, sl])
        res = jnp.zeros((rows, LANES), F32)
        for s in range(MOBA_TOPK):
            mx = jnp.max(g, axis=1, keepdims=True)
            idx = jnp.min(jnp.where(g == mx, lane, float(n_full)), axis=1, keepdims=True)
            res = jnp.where(lane_o == s, idx, res)
            g = jnp.where(lane == idx, NEG_INF, g)
        o_ref[0, h * rows:(h + 1) * rows, :] = res.astype(jnp.int32)


def _sample_topk(q_pad, kmean):
    bd, rows, w = q_pad.shape
    n_full = kmean.shape[1]
    assert n_full >= MOBA_TOPK
    kern = functools.partial(_sample_topk_kernel, n_full=n_full)
    return pl.pallas_call(
        kern, grid=(bd,),
        in_specs=[pl.BlockSpec((1, rows, w), lambda b: (b, 0, 0)),
                  pl.BlockSpec((1, n_full, w), lambda b: (b, 0, 0))],
        out_specs=pl.BlockSpec((1, N_HEADS * rows, LANES), lambda b: (b, 0, 0)),
        out_shape=jax.ShapeDtypeStruct((bd, N_HEADS * rows, LANES), jnp.int32),
        compiler_params=_params(("parallel",)),
        name="sample_topk",
    )(q_pad, kmean)


def _sample_attn_kernel(phys_ref, idx_ref, slopes_ref, q_ref, kn_ref, vn_ref, *rest,
                        tq, past, scale):
    n_slab = tq * MOBA_TOPK * 2
    k_refs = rest[:n_slab]
    v_refs = rest[n_slab:2 * n_slab]
    o_ref = rest[2 * n_slab]
    b = pl.program_id(0)
    h = pl.program_id(1)
    slope = slopes_ref[h]
    q = q_ref[0]
    rows = q.shape[0]
    qb = q.astype(BF16)
    kn = kn_ref[0]
    vn = vn_ref[0]
    r4 = lax.broadcasted_iota(jnp.int32, (rows, tq), 0)
    c4 = lax.broadcasted_iota(jnp.int32, (rows, tq), 1)
    s_own = _dot_nt(qb, kn.astype(BF16)) * scale - slope * (r4 - c4).astype(F32)
    s_own = jnp.where(c4 <= r4, s_own, NEG_INF)
    lane_blk = lax.broadcasted_iota(jnp.int32, (1, MOBA_BLOCK), 1)
    base = (b * N_HEADS + h) * tq
    for qi in range(tq):
        s_sel, v_sel = [], []
        for s in range(MOBA_TOPK):
            n = (qi * MOBA_TOPK + s) * 2
            kslab = jnp.concatenate([k_refs[n][0], k_refs[n + 1][0]], axis=0).astype(BF16)
            v_sel.append(jnp.concatenate([v_refs[n][0], v_refs[n + 1][0]], axis=0).astype(BF16))
            bidx = idx_ref[(base + qi) * MOBA_TOPK + s]
            dist = (past + qi - bidx * MOBA_BLOCK - lane_blk).astype(F32)
            sc = _dot_nt(qb, kslab)[qi:qi + 1, :] * scale - slope * dist
            s_sel.append(sc)
        so = s_own[qi:qi + 1, :]
        m = jnp.max(so, axis=1, keepdims=True)
        for sc in s_sel:
            m = jnp.maximum(m, jnp.max(sc, axis=1, keepdims=True))
        p_own = jnp.exp(so - m)
        p_sel = [jnp.exp(sc - m) for sc in s_sel]
        den = jnp.sum(p_own, axis=1, keepdims=True)
        for ps in p_sel:
            den = den + jnp.sum(ps, axis=1, keepdims=True)
        inv = 1.0 / den
        out = jnp.zeros((1, q.shape[1]), F32)
        for j in range(tq):
            out = out + (p_own[:, j:j + 1] * inv) * vn[j:j + 1, :]
        for ps, vs in zip(p_sel, v_sel):
            pb = jnp.broadcast_to((ps * inv).astype(BF16), (rows, MOBA_BLOCK))
            out = out + _dot(pb, vs)[0:1, :]
        o_ref[0, qi:qi + 1, :] = out


def _sample_attention(q_pad, k_new, v_new, cache_k3, cache_v3, phys, idx, slopes, past):
    bd, rows, w = q_pad.shape
    tq = k_new.shape[1]
    hd = w // N_HEADS
    n_slab = tq * MOBA_TOPK * 2

    def slab_spec(n):
        return pl.BlockSpec(
            (1, PAGE_SIZE, hd),
            lambda b, h, phys_r, idx_r, sl_r, n=n: (phys_r[(b * N_HEADS + h) * n_slab + n], 0, h))

    slab_specs = [slab_spec(n) for n in range(n_slab)]
    grid_spec = pltpu.PrefetchScalarGridSpec(
        num_scalar_prefetch=3,
        grid=(bd, N_HEADS),
        in_specs=[pl.BlockSpec((1, rows, hd), lambda b, h, *_: (b, 0, h)),
                  pl.BlockSpec((1, tq, hd), lambda b, h, *_: (b, 0, h)),
                  pl.BlockSpec((1, tq, hd), lambda b, h, *_: (b, 0, h))] + slab_specs + slab_specs,
        out_specs=pl.BlockSpec((1, tq, hd), lambda b, h, *_: (b, 0, h)))
    kern = functools.partial(_sample_attn_kernel, tq=tq, past=past, scale=hd ** -0.5)
    return pl.pallas_call(
        kern, grid_spec=grid_spec,
        out_shape=jax.ShapeDtypeStruct((bd, tq, w), F32),
        compiler_params=_params(("parallel", "parallel")),
        name="sample_attention",
    )(phys, idx, slopes, q_pad, k_new, v_new, *([cache_k3] * n_slab), *([cache_v3] * n_slab))


def _ssd_kernel(x_ref, b_ref, c_ref, wx_ref, wb_ref, wc_ref, bx_ref, bb_ref, bc_ref,
                ix_ref, ib_ref, ic_ref, dt_ref, dtt_ref, dtb_row_ref, dtb_col_ref,
                alog_row_ref, alog_col_ref, dskip_ref, z_ref, gssm_ref, s0_ref,
                y_ref, sfin_ref, carry_ref, cbuf_ref, ubuf_ref, *, t_valid, n_groups, d_inner):
    L = SSD_CHUNK
    gw = x_ref.shape[2]
    n_state = b_ref.shape[2]
    hpg = gw // SSM_HEADDIM
    cw = gw + 2 * n_state
    c = pl.program_id(1)
    g = pl.program_id(2)
    rows_g = pl.ds(pl.multiple_of(g * gw, gw), gw)

    @pl.when(c == 0)
    def _():
        carry_ref[g] = jnp.zeros((8, cw), F32)
        carry_ref[g, 5:8, 0:gw] = ix_ref[0]
        carry_ref[g, 5:8, gw:gw + n_state] = ib_ref[0]
        carry_ref[g, 5:8, gw + n_state:cw] = ic_ref[0]
        sfin_ref[0, rows_g, :] = s0_ref[0, rows_g, :]

    cbuf_ref[0:8, :] = carry_ref[g]
    cbuf_ref[8:8 + L, 0:gw] = x_ref[0]
    cbuf_ref[8:8 + L, gw:gw + n_state] = b_ref[0]
    cbuf_ref[8:8 + L, gw + n_state:cw] = c_ref[0]
    carry_ref[g] = cbuf_ref[L:L + 8, :]
    w = jnp.concatenate([wx_ref[...], wb_ref[...], wc_ref[...]], axis=1)
    bias = jnp.concatenate([bx_ref[...], bb_ref[...], bc_ref[...]], axis=1)
    conv = bias + w[0:1] * cbuf_ref[5:5 + L, :]
    for i in range(1, 4):
        conv = conv + w[i:i + 1] * cbuf_ref[5 + i:5 + i + L, :]
    act = _silu(conv)
    xs = act[:, 0:gw]
    bm = act[:, gw:gw + n_state].astype(BF16)
    cm = act[:, gw + n_state:cw].astype(BF16)

    dt = _softplus(dt_ref[0, 0] + dtb_row_ref[0])
    dtt = _softplus(dtt_ref[0, 0] + dtb_col_ref[0])
    if t_valid < L:
        dt = jnp.where(lax.broadcasted_iota(jnp.int32, dt.shape, 0) < t_valid, dt, 0.0)
        dtt = jnp.where(lax.broadcasted_iota(jnp.int32, dtt.shape, 1) < t_valid, dtt, 0.0)
    adt = dt * (-jnp.exp(alog_row_ref[0]))
    adtt = dtt * (-jnp.exp(alog_col_ref[0]))

    row = lax.broadcasted_iota(jnp.int32, (L, L), 0)
    col = lax.broadcasted_iota(jnp.int32, (L, L), 1)
    causal = col <= row
    tri_l = jnp.where(causal, 1.0, 0.0).astype(BF16)
    tri_u = jnp.where(row <= col, 1.0, 0.0).astype(BF16)
    acs_col = _dot_exact_lhs(tri_l, adt)
    acs_row = _dot_exact_rhs(adtt, tri_u)

    er = lax.broadcasted_iota(jnp.int32, (LANES, gw), 0)
    ec = lax.broadcasted_iota(jnp.int32, (LANES, gw), 1)
    expand = jnp.where(er == ec // SSM_HEADDIM, 1.0, 0.0).astype(BF16)
    eacs = jnp.exp(acs_col)
    dte = jnp.exp(acs_col[L - 1:L, :] - acs_col)
    stacked = _dot_exact_rhs(jnp.concatenate([dt, eacs, dte], axis=0), expand)
    dt_x = stacked[0:L]
    eacs_x = stacked[L:2 * L]
    dte_x = stacked[2 * L:3 * L]

    xdt = xs * dt_x
    xdt_b = xdt.astype(BF16)
    cb = _dot_nt(cm, bm)
    lane = lax.broadcasted_iota(jnp.int32, (L, LANES), 1)
    pairs = []
    for i in range(hpg // 2):
        xp = xdt_b[:, i * LANES:(i + 1) * LANES]
        res = []
        for j in (2 * i, 2 * i + 1):
            diff = acs_col[:, j:j + 1] - acs_row[j:j + 1, :]
            decay = jnp.exp(jnp.where(causal, diff, NEG_INF))
            res.append(_dot((cb * decay).astype(BF16), xp))
        pairs.append(jnp.where(lane < SSM_HEADDIM, res[0], res[1]))
    y_diag = jnp.concatenate(pairs, axis=1)

    s_prev = sfin_ref[0, rows_g, :]
    y_off = _dot_nt(cm, s_prev.astype(BF16)) * eacs_x
    y = y_diag + y_off + dskip_ref[...] * xs
    ubuf_ref[g] = y * _silu(z_ref[0])

    xw_t = (xdt * dte_x).T.astype(BF16)
    s_chunk = _dot(xw_t, bm)
    last = jnp.broadcast_to(acs_row[:, L - 1:L], (hpg, n_state))
    last = jnp.concatenate([last, jnp.zeros((LANES - hpg, n_state), F32)], axis=0)
    tr = lax.broadcasted_iota(jnp.int32, (gw, LANES), 0)
    tc = lax.broadcasted_iota(jnp.int32, (gw, LANES), 1)
    expand_t = jnp.where(tc == tr // SSM_HEADDIM, 1.0, 0.0).astype(BF16)
    chunk_decay = jnp.exp(_dot_exact_lhs(expand_t, last))
    sfin_ref[0, rows_g, :] = chunk_decay * s_prev + s_chunk

    @pl.when(g == n_groups - 1)
    def _():
        ss = jnp.zeros((L, 1), F32)
        for gi in range(n_groups):
            u = ubuf_ref[gi]
            ss = ss + jnp.sum(u * u, axis=1, keepdims=True)
        r = lax.rsqrt(ss * (1.0 / d_inner) + NORM_EPS)
        for gi in range(n_groups):
            yn = (ubuf_ref[gi] * r) * gssm_ref[:, gi * gw:(gi + 1) * gw]
            y_ref[0, :, gi * gw:(gi + 1) * gw] = yn.astype(y_ref.dtype)


def _ssd(xbc3, z3, dt_g, dtt_g, conv_init, ssm_init, conv_w, conv_b, dt_bias, a_log, d_skip, g_ssm,
         t_valid):
    bm, t, conv_dim = xbc3.shape
    d_inner = z3.shape[2]
    G = SSM_GROUPS
    n_state = SSM_STATE
    gw = d_inner // G
    hpg = gw // SSM_HEADDIM
    assert hpg % 2 == 0 and hpg <= 8 and n_state == LANES and 2 * SSM_HEADDIM == LANES
    L = SSD_CHUNK
    assert t % L == 0 and (t_valid == L or t == L)
    nc = t // L
    cw = gw + 2 * n_state
    xb = d_inner // n_state
    cbk = xb + G
    pad = lambda a: jnp.pad(a, ((0, 0), (0, 0), (0, LANES - a.shape[2])))
    dtb_row = pad(dt_bias.reshape(G, 1, hpg))
    dtb_col = dt_bias.reshape(G, hpg, 1)
    alog_row = pad(a_log.reshape(G, 1, hpg))
    alog_col = a_log.reshape(G, hpg, 1)
    dskip_x = jnp.repeat(d_skip, SSM_HEADDIM).reshape(1, d_inner)
    conv_b2 = conv_b.reshape(1, conv_dim)
    kern = functools.partial(_ssd_kernel, t_valid=t_valid, n_groups=G, d_inner=d_inner)
    in_specs = [
        pl.BlockSpec((1, L, gw), lambda b, c, g: (b, c, g)),
        pl.BlockSpec((1, L, n_state), lambda b, c, g: (b, c, xb + g)),
        pl.BlockSpec((1, L, n_state), lambda b, c, g: (b, c, cbk + g)),
        pl.BlockSpec((4, gw), lambda b, c, g: (0, g)),
        pl.BlockSpec((4, n_state), lambda b, c, g: (0, xb + g)),
        pl.BlockSpec((4, n_state), lambda b, c, g: (0, cbk + g)),
        pl.BlockSpec((1, gw), lambda b, c, g: (0, g)),
        pl.BlockSpec((1, n_state), lambda b, c, g: (0, xb + g)),
        pl.BlockSpec((1, n_state), lambda b, c, g: (0, cbk + g)),
        pl.BlockSpec((1, 3, gw), lambda b, c, g: (b, 0, g)),
        pl.BlockSpec((1, 3, n_state), lambda b, c, g: (b, 0, xb + g)),
        pl.BlockSpec((1, 3, n_state), lambda b, c, g: (b, 0, cbk + g)),
        pl.BlockSpec((1, 1, L, LANES), lambda b, c, g: (b, g, c, 0)),
        pl.BlockSpec((1, 1, hpg, L), lambda b, c, g: (b, g, 0, c)),
        pl.BlockSpec((1, 1, LANES), lambda b, c, g: (g, 0, 0)),
        pl.BlockSpec((1, hpg, 1), lambda b, c, g: (g, 0, 0)),
        pl.BlockSpec((1, 1, LANES), lambda b, c, g: (g, 0, 0)),
        pl.BlockSpec((1, hpg, 1), lambda b, c, g: (g, 0, 0)),
        pl.BlockSpec((1, gw), lambda b, c, g: (0, g)),
        pl.BlockSpec((1, L, gw), lambda b, c, g: (b, c, g)),
        pl.BlockSpec((1, d_inner), lambda b, c, g: (0, 0)),
        pl.BlockSpec((1, d_inner, n_state), lambda b, c, g: (b, 0, 0)),
    ]
    out_specs = [pl.BlockSpec((1, L, d_inner), lambda b, c, g: (b, c, 0)),
                 pl.BlockSpec((1, d_inner, n_state), lambda b, c, g: (b, 0, 0))]
    return pl.pallas_call(
        kern, grid=(bm, nc, G), in_specs=in_specs, out_specs=out_specs,
        out_shape=[jax.ShapeDtypeStruct((bm, t, d_inner), BF16),
                   jax.ShapeDtypeStruct((bm, d_inner, n_state), F32)],
        scratch_shapes=[pltpu.VMEM((G, 8, cw), F32), pltpu.VMEM((8 + L, cw), F32),
                        pltpu.VMEM((G, L, gw), F32)],
        compiler_params=_params(("parallel", "arbitrary", "arbitrary")),
        name="ssd",
    )(xbc3, xbc3, xbc3, conv_w, conv_w, conv_w, conv_b2, conv_b2, conv_b2,
      conv_init, conv_init, conv_init, dt_g, dtt_g, dtb_row, dtb_col, alog_row, alog_col,
      dskip_x, z3, g_ssm.reshape(1, d_inner), ssm_init)


def _merge_kernel(a_ref, y_ref, wa_ref, ws_ref, ga_ref, gs_ref, ba_ref, bs_ref, o_ref):
    pa = _dot(a_ref[...], wa_ref[...])
    ps = _dot(y_ref[...], ws_ref[...])
    o = _sigmoid(ga_ref[...] + ba_ref[...]) * pa + _sigmoid(gs_ref[...] + bs_ref[...]) * ps
    o_ref[...] = o.astype(o_ref.dtype)


def _merge(attn, y, w_a, w_s, ga, gs, b_gate, tm, tn):
    m, ka = attn.shape
    ks = y.shape[1]
    n = w_a.shape[1]
    tm = min(tm, m)
    ba = b_gate[:n].reshape(1, n)
    bs = b_gate[n:].reshape(1, n)
    return pl.pallas_call(
        _merge_kernel, grid=(m // tm, n // tn),
        in_specs=[pl.BlockSpec((tm, ka), lambda i, j: (i, 0)),
                  pl.BlockSpec((tm, ks), lambda i, j: (i, 0)),
                  pl.BlockSpec((ka, tn), lambda i, j: (0, j)),
                  pl.BlockSpec((ks, tn), lambda i, j: (0, j)),
                  pl.BlockSpec((tm, tn), lambda i, j: (i, j)),
                  pl.BlockSpec((tm, tn), lambda i, j: (i, j)),
                  pl.BlockSpec((1, tn), lambda i, j: (0, j)),
                  pl.BlockSpec((1, tn), lambda i, j: (0, j))],
        out_specs=pl.BlockSpec((tm, tn), lambda i, j: (i, j)),
        out_shape=jax.ShapeDtypeStruct((m, n), BF16),
        compiler_params=_params(("parallel", "parallel")),
        name="merge",
    )(attn, y, w_a, w_s, ga, gs, ba, bs)


def _outproj_route_kernel(mg_ref, wout_ref, x_ref, g1_ref, sc2_ref, sh2_ref, gffn_ref,
                          wgh_ref, wgl_ref, bg_ref, wrh_ref, wrl_ref, br_ref,
                          x1_ref, h2_ref, ids_ref, ew_ref, cnt_ref, carry_ref):
    i = pl.program_id(0)

    @pl.when(i == 0)
    def _():
        carry_ref[...] = jnp.zeros_like(carry_ref)

    x1 = x_ref[...] + g1_ref[0, 0] * _dot(mg_ref[...], wout_ref[...])
    x1_ref[...] = x1
    xn = x1 * lax.rsqrt(jnp.mean(x1 * x1, axis=-1, keepdims=True) + NORM_EPS)
    h2 = (xn * gffn_ref[...]) * (1.0 + sc2_ref[0, 0]) + sh2_ref[0, 0]
    h2_ref[...] = h2

    hh, hl = _split2(h2)
    lg = _dot(hh, wgh_ref[...]) + _dot(hh, wgl_ref[...]) + _dot(hl, wgh_ref[...]) + bg_ref[...]
    le = _dot(hh, wrh_ref[...]) + _dot(hh, wrl_ref[...]) + _dot(hl, wrh_ref[...]) + br_ref[...]
    tm = lg.shape[0]
    lane_i = lax.broadcasted_iota(jnp.int32, (tm, LANES), 1)
    lane = lane_i.astype(F32)
    in_g = lane_i < N_EXPERT_GROUPS
    lgm = jnp.where(in_g, lg, NEG_INF)
    mg = jnp.max(lgm, axis=1, keepdims=True)
    gtop = jnp.min(jnp.where(lgm == mg, lane, float(LANES)), axis=1, keepdims=True)
    gw = 1.0 / jnp.sum(jnp.where(in_g, jnp.exp(lg - mg), 0.0), axis=1, keepdims=True)
    grp = (lane_i // EXPERTS_PER_GROUP).astype(F32)
    lem = jnp.where((lane_i < N_EXPERTS) & (grp == gtop), le, NEG_INF)
    e1 = jnp.max(lem, axis=1, keepdims=True)
    i1 = jnp.min(jnp.where(lem == e1, lane, float(LANES)), axis=1, keepdims=True)
    lem2 = jnp.where(lane == i1, NEG_INF, lem)
    e2 = jnp.max(lem2, axis=1, keepdims=True)
    i2 = jnp.min(jnp.where(lem2 == e2, lane, float(LANES)), axis=1, keepdims=True)
    t2 = jnp.exp(e2 - e1)
    den = 1.0 + t2
    w1 = (1.0 / den) * gw
    w2 = (t2 / den) * gw

    hit1 = lane == i1
    hit2 = lane == i2
    onehot = jnp.where(hit1 | hit2, 1.0, 0.0)
    rr = lax.broadcasted_iota(jnp.int32, (tm, tm), 0)
    cc = lax.broadcasted_iota(jnp.int32, (tm, tm), 1)
    strict = jnp.where(cc < rr, 1.0, 0.0).astype(BF16)
    cum = _dot(strict, onehot.astype(BF16)) + carry_ref[...]
    r1 = jnp.sum(jnp.where(hit1, cum, 0.0), axis=1, keepdims=True)
    r2 = jnp.sum(jnp.where(hit2, cum, 0.0), axis=1, keepdims=True)
    carry_ref[...] = carry_ref[...] + jnp.sum(onehot, axis=0, keepdims=True)
    cnt_ref[...] = carry_ref[...]
    packed = jnp.where(lane_i == 0, i1, jnp.where(lane_i == 1, i2,
                       jnp.where(lane_i == 2, r1, jnp.where(lane_i == 3, r2, 0.0))))
    ids_ref[...] = packed.astype(jnp.int32)
    ew_ref[...] = jnp.where(lane_i == 0, w1, jnp.where(lane_i == 1, w2, 0.0))


def _outproj_route(merged, w_out, x2, mod4, t_seq, g_ffn, w_group, b_group, w_router, b_router, tm):
    m, d = x2.shape
    r = mod4.shape[2]
    assert t_seq % tm == 0 and (r == 1 or tm == t_seq)
    per_seq = t_seq // tm

    def pad_cols(w):
        return jnp.pad(w, ((0, 0), (0, LANES - w.shape[1])))

    wg = pad_cols(w_group)
    wr = pad_cols(w_router)
    wgh, wgl = _split2(wg)
    wrh, wrl = _split2(wr)
    bg = pad_cols(b_group.reshape(1, -1))
    br = pad_cols(b_router.reshape(1, -1))
    row_spec = pl.BlockSpec((tm, d), lambda i: (i, 0))
    full = lambda shape: pl.BlockSpec(shape, lambda i: tuple(0 for _ in shape))
    mod_spec = lambda idx: pl.BlockSpec((1, 1, r, d), lambda i: (idx, i // per_seq, 0, 0))
    lane_spec = pl.BlockSpec((tm, LANES), lambda i: (i, 0))
    return pl.pallas_call(
        _outproj_route_kernel, grid=(m // tm,),
        in_specs=[row_spec, full((d, d)), row_spec, mod_spec(2), mod_spec(4), mod_spec(3), full((1, d)),
                  full((d, LANES)), full((d, LANES)), full((1, LANES)),
                  full((d, LANES)), full((d, LANES)), full((1, LANES))],
        out_specs=[row_spec, row_spec, lane_spec, lane_spec, full((1, LANES))],
        out_shape=[jax.ShapeDtypeStruct((m, d), F32), jax.ShapeDtypeStruct((m, d), F32),
                   jax.ShapeDtypeStruct((m, LANES), jnp.int32), jax.ShapeDtypeStruct((m, LANES), F32),
                   jax.ShapeDtypeStruct((1, LANES), F32)],
        scratch_shapes=[pltpu.VMEM((1, LANES), F32)],
        compiler_params=_params(("arbitrary",)),
        name="outproj_route",
    )(merged, w_out, x2, mod4, mod4, mod4, g_ffn.reshape(1, d), wgh, wgl, bg, wrh, wrl, br)


def _row_copy(src_hbm, row, dst, slot, sem):
    return pltpu.make_async_copy(src_hbm.at[row], dst.at[slot], sem)


def _expert_kernel(bexp_ref, nb_ref, tok_ref, h2_hbm, sw_ref, wg_ref, wu_ref, wd_ref, o_ref, xbuf, sem):
    blk = pl.program_id(0)
    n_chunks = xbuf.shape[1]

    @pl.when(blk < nb_ref[0])
    def _():
        base = blk * MOE_BLOCK

        def issue(r, carry):
            _row_copy(h2_hbm, tok_ref[base + r], xbuf, r, sem).start()
            return carry

        lax.fori_loop(0, MOE_BLOCK, issue, 0)

        def drain(r, carry):
            _row_copy(h2_hbm, 0, xbuf, r, sem).wait()
            return carry

        lax.fori_loop(0, MOE_BLOCK, drain, 0)
        x = jnp.concatenate([xbuf[:, s, :] for s in range(n_chunks)], axis=1).astype(BF16)
        gate = _dot(x, wg_ref[0])
        up = _dot(x, wu_ref[0])
        act = (_silu(gate) * up).astype(BF16)
        y = _dot(act, wd_ref[0]) * sw_ref[...]
        for s in range(n_chunks):
            o_ref[:, s, :] = y[:, s * LANES:(s + 1) * LANES]

    @pl.when(blk >= nb_ref[0])
    def _():
        o_ref[...] = jnp.zeros_like(o_ref)


def _experts(h2, slot_tok, slot_w, block_expert, n_used, w_gate_b, w_up_b, w_down_b):
    t, d = h2.shape
    n_slots = slot_tok.shape[0]
    n_blocks = n_slots // MOE_BLOCK
    n_chunks = d // LANES
    hid = w_up_b.shape[2]
    grid_spec = pltpu.PrefetchScalarGridSpec(
        num_scalar_prefetch=3,
        grid=(n_blocks,),
        in_specs=[pl.BlockSpec(memory_space=pl.ANY),
                  pl.BlockSpec((MOE_BLOCK, 1), lambda i, be, nb, tok: (i, 0)),
                  pl.BlockSpec((1, d, hid), lambda i, be, nb, tok: (be[i], 0, 0)),
                  pl.BlockSpec((1, d, hid), lambda i, be, nb, tok: (be[i], 0, 0)),
                  pl.BlockSpec((1, hid, d), lambda i, be, nb, tok: (be[i], 0, 0))],
        out_specs=pl.BlockSpec((MOE_BLOCK, n_chunks, LANES), lambda i, be, nb, tok: (i, 0, 0)),
        scratch_shapes=[pltpu.VMEM((MOE_BLOCK, n_chunks, LANES), F32), pltpu.SemaphoreType.DMA])
    return pl.pallas_call(
        _expert_kernel, grid_spec=grid_spec,
        out_shape=jax.ShapeDtypeStruct((n_slots, n_chunks, LANES), F32),
        compiler_params=_params(("arbitrary",)),
        name="experts",
    )(block_expert, n_used, slot_tok, h2.reshape(t, n_chunks, LANES), slot_w, w_gate_b, w_up_b, w_down_b)


def _combine_kernel(dest_ref, yb_hbm, x1_ref, g2_ref, gfin_ref, o_ref, buf, sem):
    i = pl.program_id(0)
    tm = x1_ref.shape[0]
    n_rows = buf.shape[0]
    n_chunks = buf.shape[1]
    base = i * n_rows

    def issue(r, carry):
        _row_copy(yb_hbm, dest_ref[base + r], buf, r, sem).start()
        return carry

    lax.fori_loop(0, n_rows, issue, 0)

    def drain(r, carry):
        _row_copy(yb_hbm, 0, buf, r, sem).wait()
        return carry

    lax.fori_loop(0, n_rows, drain, 0)
    moe = jnp.concatenate([buf[0:tm, s, :] + buf[tm:2 * tm, s, :] for s in range(n_chunks)], axis=1)
    x2 = x1_ref[...] + g2_ref[0, 0] * moe
    xn = x2 * lax.rsqrt(jnp.mean(x2 * x2, axis=-1, keepdims=True) + NORM_EPS)
    o_ref[...] = xn * gfin_ref[...]


def _combine(yb, dest_tiles, x1, mod4, t_seq, g_final, tm):
    m, d = x1.shape
    r = mod4.shape[2]
    assert t_seq % tm == 0 and (r == 1 or tm == t_seq)
    per_seq = t_seq // tm
    n_chunks = d // LANES
    grid_spec = pltpu.PrefetchScalarGridSpec(
        num_scalar_prefetch=1,
        grid=(m // tm,),
        in_specs=[pl.BlockSpec(memory_space=pl.ANY),
                  pl.BlockSpec((tm, d), lambda i, de: (i, 0)),
                  pl.BlockSpec((1, 1, r, d), lambda i, de: (5, i // per_seq, 0, 0)),
                  pl.BlockSpec((1, d), lambda i, de: (0, 0))],
        out_specs=pl.BlockSpec((tm, d), lambda i, de: (i, 0)),
        scratch_shapes=[pltpu.VMEM((2 * tm, n_chunks, LANES), F32), pltpu.SemaphoreType.DMA])
    return pl.pallas_call(
        _combine_kernel, grid_spec=grid_spec,
        out_shape=jax.ShapeDtypeStruct((m, d), F32),
        compiler_params=_params(("arbitrary",)),
        name="combine",
    )(dest_tiles, yb, x1, mod4, g_final.reshape(1, d))


def _moe_and_final_norm(merged, x2, mod4, t_seq, lw, g_final, tm_route, tm_comb):
    m, d = x2.shape
    x1, h2, ids, ew, cnt = _outproj_route(merged, lw["w_out_b"], x2, mod4, t_seq, lw["g_ffn"],
                                          lw["w_group"], lw["b_group"], lw["w_router"], lw["b_router"],
                                          tm_route)
    e_id = ids[:, 0:2]
    rank = ids[:, 2:4]
    e_w = ew[:, 0:2]
    counts = cnt[0, :N_EXPERTS].astype(jnp.int32)
    padded = (counts + MOE_BLOCK - 1) // MOE_BLOCK * MOE_BLOCK
    pad_end = jnp.cumsum(padded)
    dest = (pad_end - padded)[e_id] + rank
    n_assign = 2 * m
    n_blocks = (n_assign + N_EXPERTS * (MOE_BLOCK - 1) + MOE_BLOCK - 1) // MOE_BLOCK
    n_slots = n_blocks * MOE_BLOCK
    tok = jnp.broadcast_to(jnp.arange(m, dtype=jnp.int32)[:, None], (m, 2))
    slot_tok = jnp.zeros((n_slots,), jnp.int32).at[dest.reshape(-1)].set(tok.reshape(-1))
    slot_w = jnp.zeros((n_slots,), F32).at[dest.reshape(-1)].set(e_w.reshape(-1))
    block_expert = jnp.minimum(
        jnp.searchsorted(pad_end, jnp.arange(n_blocks, dtype=jnp.int32) * MOE_BLOCK, side="right"),
        N_EXPERTS - 1).astype(jnp.int32)
    n_used = (pad_end[-1:] // MOE_BLOCK).astype(jnp.int32)
    yb = _experts(h2, slot_tok, slot_w.reshape(n_slots, 1), block_expert, n_used,
                  lw["w_gate_e_b"], lw["w_up_b"], lw["w_down_b"])
    dest_tiles = dest.reshape(m // tm_comb, tm_comb, 2).transpose(0, 2, 1).reshape(-1).astype(jnp.int32)
    return _combine(yb, dest_tiles, x1, mod4, t_seq, g_final, tm_comb)


def _input_projections(h1, lw, tm):
    proj = {name: _matmul(h1, lw["w_" + name], tm, 512) for name in ("q", "k", "v", "z", "xbc", "ga", "gs")}
    proj["dt"] = _matmul(h1, lw["w_dt"], tm, LANES)
    return proj


def _dt_layouts(dt_raw, bm, t, t_pad):
    hpg = HEADS_PER_GROUP
    d = dt_raw[:, :SSM_GROUPS * hpg].reshape(bm, t, SSM_GROUPS, hpg)
    d = jnp.pad(d, ((0, 0), (0, t_pad - t), (0, 0), (0, 0)))
    dt_g = jnp.pad(d.transpose(0, 2, 1, 3), ((0, 0), (0, 0), (0, 0), (0, LANES - hpg)))
    dtt_g = d.transpose(0, 2, 3, 1)
    return dt_g, dtt_g


def kernel(x_prompt, x_sample, cache_k, cache_v, state_ssm, state_conv, page_table, c_prompt, c_sample,
           w_ada, b_ada, g_mix, w_in, b_gate, conv_w, conv_b, dt_bias, a_log, d_skip, g_ssm,
           w_attn_br, w_ssm_br, w_out, g_ffn, w_group, b_group, w_router, b_router,
           w_up, w_gate_e, w_down, g_final):
    depth = w_ada.shape[0]
    assert depth == 1
    bp, t, d = x_prompt.shape
    bd, tq, _ = x_sample.shape
    att_w = N_HEADS * (d // N_HEADS)
    d_inner = g_ssm.shape[1]
    conv_dim = conv_w.shape[2]
    n_ssm_heads = dt_bias.shape[1]
    assert n_ssm_heads == SSM_GROUPS * HEADS_PER_GROUP and d_inner == n_ssm_heads * SSM_HEADDIM
    n_pages = page_table.shape[1]
    past = n_pages * PAGE_SIZE
    assert past % MOBA_BLOCK == 0 and past > 0
    slopes = 2.0 ** (-8.0 * jnp.arange(1, N_HEADS + 1, dtype=F32) / N_HEADS)

    l = 0
    w_in_b = w_in[l].astype(BF16)
    cuts = [0, att_w, 2 * att_w, 3 * att_w, 3 * att_w + d_inner, 3 * att_w + d_inner + conv_dim,
            3 * att_w + d_inner + conv_dim + n_ssm_heads]
    cuts += [cuts[-1] + d, cuts[-1] + 2 * d]
    names = ("q", "k", "v", "z", "xbc", "dt", "ga", "gs")
    lw = {"w_" + n: w_in_b[:, cuts[i]:cuts[i + 1]] for i, n in enumerate(names)}
    lw["w_dt"] = jnp.pad(lw["w_dt"], ((0, 0), (0, LANES - n_ssm_heads)))
    lw.update(w_attn_b=w_attn_br[l].astype(BF16), w_ssm_b=w_ssm_br[l].astype(BF16),
              w_out_b=w_out[l].astype(BF16), g_ffn=g_ffn[l], w_group=w_group[l], b_group=b_group[l],
              w_router=w_router[l], b_router=b_router[l], w_up_b=w_up[l].astype(BF16),
              w_gate_e_b=w_gate_e[l].astype(BF16), w_down_b=w_down[l].astype(BF16))

    mod = _ada_mod(jnp.concatenate([c_prompt, c_sample], axis=0), w_ada[l], b_ada[l])
    mod = mod.reshape(bp + bd, 6, d)
    mod_p = mod[:bp].transpose(1, 0, 2).reshape(6, bp, 1, d)
    mod_s = jnp.repeat(mod[bp:].transpose(1, 0, 2), tq, axis=1).reshape(6, 1, bd * tq, d)

    ssd_args = (conv_w[l], conv_b[l], dt_bias[l], a_log[l], d_skip[l], g_ssm[l])

    mp = bp * t
    h1 = _modnorm(x_prompt, g_mix[l], mod_p, 1, 0, 512)
    pj = _input_projections(h1, lw, 1024)
    q3, k3, v3 = (pj[n].reshape(bp, t, att_w) for n in ("q", "k", "v"))
    attn = _moba_prompt(q3, k3, v3, slopes).reshape(mp, att_w)
    xbc3 = pj["xbc"].reshape(bp, t, conv_dim)
    dt_g, dtt_g = _dt_layouts(pj["dt"], bp, t, t)
    y, ssm_p = _ssd(xbc3, pj["z"].reshape(bp, t, d_inner), dt_g, dtt_g,
                    jnp.zeros((bp, 3, conv_dim), F32), jnp.zeros((bp, d_inner, SSM_STATE), F32),
                    *ssd_args, t_valid=SSD_CHUNK)
    merged = _merge(attn, y.reshape(mp, d_inner), lw["w_attn_b"], lw["w_ssm_b"], pj["ga"], pj["gs"],
                    b_gate[l], 512, 512)
    y_prompt = _moe_and_final_norm(merged, x_prompt.reshape(mp, d), mod_p, t, lw, g_final, 256, 128)
    y_prompt = y_prompt.reshape(bp, t, d)
    k_p = k3.reshape(1, bp, t, N_HEADS, att_w // N_HEADS)
    v_p = v3.reshape(1, bp, t, N_HEADS, att_w // N_HEADS)
    conv_p = xbc3[:, t - 3:, :][None]
    ssm_p = ssm_p.reshape(1, bp, n_ssm_heads, SSM_HEADDIM, SSM_STATE)

    ms = bd * tq
    xs3 = x_sample.reshape(1, ms, d)
    h1s = _modnorm(xs3, g_mix[l], mod_s, 1, 0, ms)
    pjs = _input_projections(h1s, lw, ms)
    qs, ks, vs = (pjs[n].reshape(bd, tq, att_w) for n in ("q", "k", "v"))
    n_phys = cache_k.shape[1]
    cache_k3 = cache_k[l].reshape(n_phys, PAGE_SIZE, att_w)
    cache_v3 = cache_v[l].reshape(n_phys, PAGE_SIZE, att_w)
    kmean = _cache_block_means(cache_k3, page_table)
    q_pad = jnp.pad(qs, ((0, 0), (0, 8 - tq), (0, 0)))
    top = _sample_topk(q_pad, kmean).reshape(bd, N_HEADS, 8, LANES)[:, :, :tq, :MOBA_TOPK]
    logical = top[..., None] * 2 + jnp.arange(2, dtype=jnp.int32)
    phys = page_table[jnp.arange(bd)[:, None, None, None, None], logical]
    attn_s = _sample_attention(q_pad, ks, vs, cache_k3, cache_v3, phys.reshape(-1).astype(jnp.int32),
                               top.reshape(-1), slopes, past)
    attn_s = attn_s.reshape(ms, att_w).astype(BF16)
    xbc_s = pjs["xbc"].reshape(bd, tq, conv_dim)
    pad_t = lambda a: jnp.pad(a, ((0, 0), (0, SSD_CHUNK - tq), (0, 0)))
    dt_gs, dtt_gs = _dt_layouts(pjs["dt"], bd, tq, SSD_CHUNK)
    y_s, ssm_s = _ssd(pad_t(xbc_s), pad_t(pjs["z"].reshape(bd, tq, d_inner)), dt_gs, dtt_gs,
                      state_conv[l], state_ssm[l].reshape(bd, d_inner, SSM_STATE), *ssd_args, t_valid=tq)
    y_s = y_s[:, :tq, :].reshape(ms, d_inner)
    merged_s = _merge(attn_s, y_s, lw["w_attn_b"], lw["w_ssm_b"], pjs["ga"], pjs["gs"], b_gate[l], ms, 512)
    y_sample = _moe_and_final_norm(merged_s, x_sample.reshape(ms, d), mod_s, ms, lw, g_final, ms, ms)
    y_sample = y_sample.reshape(bd, tq, d)
    k_s = ks.reshape(1, bd, tq, N_HEADS, att_w // N_HEADS)
    v_s = vs.reshape(1, bd, tq, N_HEADS, att_w // N_HEADS)
    conv_s = jnp.concatenate([state_conv[l], xbc_s], axis=1)[:, tq:, :][None]
    ssm_s = ssm_s.reshape(1, bd, n_ssm_heads, SSM_HEADDIM, SSM_STATE)

    return (y_prompt, y_sample, k_p, v_p, k_s, v_s, ssm_p, ssm_s, conv_p, conv_s)
```

```python
import functools

import jax
import jax.numpy as jnp
from jax import lax
from jax.experimental import pallas as pl
from jax.experimental.pallas import tpu as pltpu

F32 = jnp.float32
BF16 = jnp.bfloat16
NEG_INF = float("-inf")

PAGE_SIZE = 128
N_HEADS = 16
MOBA_BLOCK = 256
MOBA_TOPK = 3
SSM_HEADDIM = 64
SSM_GROUPS = 8
SSM_STATE = 128
HEADS_PER_GROUP = 8
SSD_CHUNK = 128
N_EXPERT_GROUPS = 4
EXPERTS_PER_GROUP = 8
N_EXPERTS = N_EXPERT_GROUPS * EXPERTS_PER_GROUP
MOE_BLOCK = 128
NORM_EPS = 1e-6

LANES = 128
VMEM_LIMIT_MB = 56


def _params(semantics, vmem_mb=VMEM_LIMIT_MB):
    return pltpu.CompilerParams(dimension_semantics=semantics,
                                vmem_limit_bytes=vmem_mb * 1024 * 1024)


def _sigmoid(x):
    return 0.5 * jnp.tanh(0.5 * x) + 0.5


def _silu(x):
    return x * _sigmoid(x)


def _softplus(x):
    return jnp.maximum(x, 0.0) + jnp.log1p(jnp.exp(-jnp.abs(x)))


def _dot(a, b):
    return jnp.dot(a, b, preferred_element_type=F32)


def _dot_nt(a, b):
    return lax.dot_general(a, b, (((1,), (1,)), ((), ())), preferred_element_type=F32)


def _split2(x):
    hi = x.astype(BF16)
    lo = (x - hi.astype(F32)).astype(BF16)
    return hi, lo


def _split3(x):
    p1 = x.astype(BF16)
    r1 = x - p1.astype(F32)
    p2 = r1.astype(BF16)
    p3 = (r1 - p2.astype(F32)).astype(BF16)
    return p1, p2, p3


def _dot_exact_rhs(x, e):
    p1, p2, p3 = _split3(x)
    return _dot(p1, e) + _dot(p2, e) + _dot(p3, e)


def _dot_exact_lhs(e, x):
    p1, p2, p3 = _split3(x)
    return _dot(e, p1) + _dot(e, p2) + _dot(e, p3)


def _dot_hi_nt(a, b):
    ah, al = _split2(a)
    bh, bl = _split2(b)
    return _dot_nt(ah, bh) + _dot_nt(ah, bl) + _dot_nt(al, bh)


def _ada_kernel(c_ref, w_ref, b_ref, o_ref):
    a = _silu(c_ref[...]).astype(BF16)
    o_ref[...] = _dot(a, w_ref[...].astype(BF16)) + b_ref[...]


def _ada_mod(c, w_ada, b_ada):
    mc, d = c.shape
    n = w_ada.shape[1]
    tn = 1024
    return pl.pallas_call(
        _ada_kernel,
        grid=(n // tn,),
        in_specs=[pl.BlockSpec((mc, d), lambda j: (0, 0)),
                  pl.BlockSpec((d, tn), lambda j: (0, j)),
                  pl.BlockSpec((1, tn), lambda j: (0, j))],
        out_specs=pl.BlockSpec((mc, tn), lambda j: (0, j)),
        out_shape=jax.ShapeDtypeStruct((mc, n), F32),
        compiler_params=_params(("parallel",)),
        name="ada_mod",
    )(c, w_ada, b_ada.reshape(1, n))


def _modnorm_kernel(x_ref, g_ref, sc_ref, sh_ref, o_ref):
    x = x_ref[0]
    xn = x * lax.rsqrt(jnp.mean(x * x, axis=-1, keepdims=True) + NORM_EPS)
    h = (xn * g_ref[...]) * (1.0 + sc_ref[0, 0]) + sh_ref[0, 0]
    o_ref[...] = h.astype(o_ref.dtype)


def _modnorm(x3, g, mod4, scale_idx, shift_idx, tt):
    bm, t, d = x3.shape
    r = mod4.shape[2]
    assert r == 1 or (r == t and tt == t)
    nt = t // tt
    return pl.pallas_call(
        _modnorm_kernel,
        grid=(bm, nt),
        in_specs=[pl.BlockSpec((1, tt, d), lambda b, i: (b, i, 0)),
                  pl.BlockSpec((1, d), lambda b, i: (0, 0)),
                  pl.BlockSpec((1, 1, r, d), lambda b, i: (scale_idx, b, 0, 0)),
                  pl.BlockSpec((1, 1, r, d), lambda b, i: (shift_idx, b, 0, 0))],
        out_specs=pl.BlockSpec((tt, d), lambda b, i: (b * nt + i, 0)),
        out_shape=jax.ShapeDtypeStruct((bm * t, d), BF16),
        compiler_params=_params(("parallel", "parallel")),
        name="modnorm",
    )(x3, g.reshape(1, d), mod4, mod4)


def _mm_kernel(a_ref, w_ref, o_ref):
    o_ref[...] = _dot(a_ref[...], w_ref[...]).astype(o_ref.dtype)


def _matmul(a, w, tm, tn, out_dtype=F32):
    m, k = a.shape
    n = w.shape[1]
    tm = min(tm, m)
    tn = min(tn, n)
    return pl.pallas_call(
        _mm_kernel,
        grid=(m // tm, n // tn),
        in_specs=[pl.BlockSpec((tm, k), lambda i, j: (i, 0)),
                  pl.BlockSpec((k, tn), lambda i, j: (0, j))],
        out_specs=pl.BlockSpec((tm, tn), lambda i, j: (i, j)),
        out_shape=jax.ShapeDtypeStruct((m, n), out_dtype),
        compiler_params=_params(("parallel", "parallel")),
        name="matmul",
    )(a, w)


def _moba_prompt_kernel(slopes_ref, q_ref, k_ref, v_ref, o_ref,
                        kb_ref, vb_ref, km_ref, bias_ref, bias_own_ref, s_ref, *, n_blk, scale):
    h = pl.program_id(1)
    ob = pl.program_id(2)
    blk = MOBA_BLOCK
    slope = slopes_ref[h]

    @pl.when(ob == 0)
    def _():
        k = k_ref[0]
        kb_ref[...] = k.astype(BF16)
        vb_ref[...] = v_ref[0].astype(BF16)
        for j in range(n_blk):
            km_ref[j:j + 1, :] = jnp.sum(k[j * blk:(j + 1) * blk], axis=0, keepdims=True) * (1.0 / blk)
        row = lax.broadcasted_iota(jnp.int32, (blk, blk), 0)
        col = lax.broadcasted_iota(jnp.int32, (blk, blk), 1)
        b0 = slope * (row - col).astype(F32)
        bias_ref[...] = b0
        bias_own_ref[...] = jnp.where(col <= row, b0, float("inf"))

    q = q_ref[0]
    qb = q.astype(BF16)

    gt = _dot_hi_nt(km_ref[...], q)
    jrow = lax.broadcasted_iota(jnp.int32, gt.shape, 0)
    gt = jnp.where(jrow < ob, gt, NEG_INF)
    pen_rows = []
    for j in range(n_blk):
        gj = gt[j:j + 1, :]
        beats = (gt > gj) | ((gt == gj) & (jrow < j))
        rank = jnp.sum(jnp.where(beats, 1.0, 0.0), axis=0, keepdims=True)
        pen_rows.append(jnp.where(rank < MOBA_TOPK, 0.0, NEG_INF))
    pen_rows.append(jnp.zeros((LANES - n_blk, blk), F32))
    pen = jnp.concatenate(pen_rows, axis=0).T

    def attend(n):
        mx = None
        for jj in range(n + 1):
            qk = _dot_nt(qb, kb_ref[jj * blk:(jj + 1) * blk, :]) * scale
            if jj == n:
                s = qk - bias_own_ref[...]
            else:
                s = (qk - bias_ref[...]) + (pen[:, jj:jj + 1] - slope * float((n - jj) * blk))
            s_ref[jj] = s
            part = jnp.maximum(s[:, :LANES], s[:, LANES:])
            mx = part if mx is None else jnp.maximum(mx, part)
        m = jnp.max(mx, axis=1, keepdims=True)
        lsum = None
        acc = None
        for jj in range(n + 1):
            p = jnp.exp(s_ref[jj] - m)
            part = p[:, :LANES] + p[:, LANES:]
            lsum = part if lsum is None else lsum + part
            pv = _dot(p.astype(BF16), vb_ref[jj * blk:(jj + 1) * blk, :])
            acc = pv if acc is None else acc + pv
        l = jnp.sum(lsum, axis=1, keepdims=True)
        o_ref[0] = (acc / l).astype(o_ref.dtype)

    for n in range(n_blk):
        pl.when(ob == n)(functools.partial(attend, n))


def _moba_prompt(q3, k3, v3, slopes):
    b, t, w = q3.shape
    hd = w // N_HEADS
    assert t % MOBA_BLOCK == 0 and hd == LANES
    n_blk = t // MOBA_BLOCK
    assert n_blk >= MOBA_TOPK
    kern = functools.partial(_moba_prompt_kernel, n_blk=n_blk, scale=hd ** -0.5)
    grid_spec = pltpu.PrefetchScalarGridSpec(
        num_scalar_prefetch=1,
        grid=(b, N_HEADS, n_blk),
        in_specs=[pl.BlockSpec((1, MOBA_BLOCK, hd), lambda bi, h, i, s: (bi, i, h)),
                  pl.BlockSpec((1, t, hd), lambda bi, h, i, s: (bi, 0, h)),
                  pl.BlockSpec((1, t, hd), lambda bi, h, i, s: (bi, 0, h))],
        out_specs=pl.BlockSpec((1, MOBA_BLOCK, hd), lambda bi, h, i, s: (bi, i, h)),
        scratch_shapes=[pltpu.VMEM((t, hd), BF16), pltpu.VMEM((t, hd), BF16),
                        pltpu.VMEM((n_blk, hd), F32),
                        pltpu.VMEM((MOBA_BLOCK, MOBA_BLOCK), F32), pltpu.VMEM((MOBA_BLOCK, MOBA_BLOCK), F32),
                        pltpu.VMEM((n_blk, MOBA_BLOCK, MOBA_BLOCK), F32)])
    return pl.pallas_call(
        kern, grid_spec=grid_spec,
        out_shape=jax.ShapeDtypeStruct((b, t, w), BF16),
        compiler_params=_params(("parallel", "parallel", "arbitrary")),
        name="moba_prompt",
    )(slopes, q3, k3, v3)


BLOCKS_PER_MEAN_STEP = 2


def _kmean_kernel(pt_ref, *refs):
    page_refs, o_ref = refs[:-1], refs[-1]
    ppb = MOBA_BLOCK // PAGE_SIZE
    for i in range(BLOCKS_PER_MEAN_STEP):
        s = jnp.sum(page_refs[ppb * i][0, 0], axis=0)
        for p in range(1, ppb):
            s = s + jnp.sum(page_refs[ppb * i + p][0, 0], axis=0)
        o_ref[0, i] = s * (1.0 / MOBA_BLOCK)


def _cache_block_means(cache_k5, layer, page_table):
    bd, n_pages = page_table.shape
    _, _, page, nh, hd = cache_k5.shape
    ppb = MOBA_BLOCK // page
    n_full = n_pages // ppb
    assert n_full % BLOCKS_PER_MEAN_STEP == 0
    pps = ppb * BLOCKS_PER_MEAN_STEP

    def page_spec(i):
        return pl.BlockSpec((1, 1, page, nh, hd),
                            lambda b, j, pt, i=i: (layer, pt[b * n_pages + pps * j + i], 0, 0, 0))

    grid_spec = pltpu.PrefetchScalarGridSpec(
        num_scalar_prefetch=1,
        grid=(bd, n_full // BLOCKS_PER_MEAN_STEP),
        in_specs=[page_spec(i) for i in range(pps)],
        out_specs=pl.BlockSpec((1, BLOCKS_PER_MEAN_STEP, nh, hd), lambda b, j, pt: (b, j, 0, 0)))
    return pl.pallas_call(
        _kmean_kernel, grid_spec=grid_spec,
        out_shape=jax.ShapeDtypeStruct((bd, n_full, nh, hd), F32),
        compiler_params=_params(("parallel", "parallel")),
        name="cache_block_means",
    )(page_table.reshape(-1), *([cache_k5] * pps))


def _sample_topk_kernel(q_ref, km_ref, o_ref, *, n_full):
    q = q_ref[0]
    rows = q.shape[0]
    lane = lax.broadcasted_iota(jnp.int32, (rows, n_full), 1).astype(F32)
    lane_o = lax.broadcasted_iota(jnp.int32, (rows, LANES), 1)
    for h in range(N_HEADS):
        g = _dot_hi_nt(q[:, h * LANES:(h + 1) * LANES], km_ref[0, :, h, :])
        res = jnp.zeros((rows, LANES), F32)
        for s in range(MOBA_TOPK):
            mx = jnp.max(g, axis=1, keepdims=True)
            idx = jnp.min(jnp.where(g == mx, lane, float(n_full)), axis=1, keepdims=True)
            res = jnp.where(lane_o == s, idx, res)
            g = jnp.where(lane == idx, NEG_INF, g)
        o_ref[0, h * rows:(h + 1) * rows, :] = res.astype(jnp.int32)


def _sample_topk(q_pad, kmean):
    bd, rows, w = q_pad.shape
    _, n_full, nh, hd = kmean.shape
    assert n_full >= MOBA_TOPK and nh == N_HEADS and hd == LANES
    kern = functools.partial(_sample_topk_kernel, n_full=n_full)
    return pl.pallas_call(
        kern, grid=(bd,),
        in_specs=[pl.BlockSpec((1, rows, w), lambda b: (b, 0, 0)),
                  pl.BlockSpec((1, n_full, nh, hd), lambda b: (b, 0, 0, 0))],
        out_specs=pl.BlockSpec((1, N_HEADS * rows, LANES), lambda b: (b, 0, 0)),
        out_shape=jax.ShapeDtypeStruct((bd, N_HEADS * rows, LANES), jnp.int32),
        compiler_params=_params(("parallel",)),
        name="sample_topk",
    )(q_pad, kmean)


def _sample_attn_kernel(pt_ref, sel_ref, slopes_ref, q_ref, kn_ref, vn_ref, *rest,
                        tq, past, n_full, scale):
    ppb = MOBA_BLOCK // PAGE_SIZE
    k_refs = rest[:ppb]
    v_refs = rest[ppb:2 * ppb]
    o_ref, qs_ref, bias_ref, m_ref, l_ref, acc_ref = rest[2 * ppb:]
    b = pl.program_id(0)
    j = pl.program_id(1)
    rows = q_ref.shape[1]
    n_rows = N_HEADS * rows
    n_keys = MOBA_BLOCK * N_HEADS
    head_of_row = lax.broadcasted_iota(jnp.int32, (n_rows, 1), 0) // rows
    q_of_row = lax.broadcasted_iota(jnp.int32, (n_rows, 1), 0) % rows
    slope_col = jnp.zeros((n_rows, 1), F32)
    for h in range(N_HEADS):
        slope_col = jnp.where(head_of_row == h, slopes_ref[h], slope_col)

    @pl.when(j == 0)
    def _():
        q = q_ref[0]
        r4 = lax.broadcasted_iota(jnp.int32, (rows, tq), 0)
        c4 = lax.broadcasted_iota(jnp.int32, (rows, tq), 1)
        for h in range(N_HEADS):
            sl = slice(h * LANES, (h + 1) * LANES)
            hs = slice(h * rows, (h + 1) * rows)
            qs_ref[hs, :] = q[:, sl].astype(BF16)
            kn = kn_ref[0, :, sl]
            vn = vn_ref[0, :, sl]
            s = _dot_nt(q[:, sl].astype(BF16), kn.astype(BF16)) * scale - slopes_ref[h] * (r4 - c4).astype(F32)
            s = jnp.where(c4 <= r4, s, NEG_INF)
            m = jnp.max(s, axis=1, keepdims=True)
            p = jnp.exp(s - m)
            acc = p[:, 0:1] * vn[0:1, :]
            for c in range(1, tq):
                acc = acc + p[:, c:c + 1] * vn[c:c + 1, :]
            m_ref[hs, :] = m
            l_ref[hs, :] = jnp.sum(p, axis=1, keepdims=True)
            acc_ref[hs, :] = acc
        key = lax.broadcasted_iota(jnp.int32, (n_rows, n_keys), 1)
        rel = (q_of_row - key // N_HEADS).astype(F32)
        bias_ref[...] = jnp.where(key % N_HEADS == head_of_row, slope_col * rel, float("inf"))

    bits_col = jnp.zeros((n_rows, 1), jnp.int32)
    for h in range(N_HEADS):
        bits_col = jnp.where(head_of_row == h, sel_ref[(b * N_HEADS + h) * n_full + j], bits_col)
    picked = ((bits_col >> q_of_row) & 1) == 1
    offset = (past - j * MOBA_BLOCK).astype(F32)
    t_col = jnp.where(picked, -(slope_col * offset), NEG_INF)

    kp = jnp.concatenate([r[0, 0].reshape(PAGE_SIZE * N_HEADS, LANES) for r in k_refs], axis=0).astype(BF16)
    vp = jnp.concatenate([r[0, 0].reshape(PAGE_SIZE * N_HEADS, LANES) for r in v_refs], axis=0).astype(BF16)
    s = (_dot_nt(qs_ref[...], kp) * scale - bias_ref[...]) + t_col
    m_old = m_ref[...]
    m_new = jnp.maximum(m_old, jnp.max(s, axis=1, keepdims=True))
    alpha = jnp.exp(m_old - m_new)
    p = jnp.exp(s - m_new)
    l_ref[...] = alpha * l_ref[...] + jnp.sum(p, axis=1, keepdims=True)
    acc_ref[...] = alpha * acc_ref[...] + _dot(p.astype(BF16), vp)
    m_ref[...] = m_new

    @pl.when(j == n_full - 1)
    def _():
        for h in range(N_HEADS):
            hs = slice(h * rows, (h + 1) * rows)
            o_ref[0, :, h * LANES:(h + 1) * LANES] = acc_ref[hs, :] / l_ref[hs, :]


def _sample_attention(q_pad, k_new, v_new, cache_k5, cache_v5, layer, page_table, sel_bits, slopes):
    bd, rows, w = q_pad.shape
    tq = k_new.shape[1]
    n_pages = page_table.shape[1]
    _, _, page, nh, hd = cache_k5.shape
    ppb = MOBA_BLOCK // page
    n_full = n_pages // ppb
    assert nh == N_HEADS and hd == LANES and n_pages % ppb == 0

    def page_spec(i):
        return pl.BlockSpec((1, 1, page, nh, hd),
                            lambda b, j, pt, sel, sl, i=i: (layer, pt[b * n_pages + ppb * j + i], 0, 0, 0))

    grid_spec = pltpu.PrefetchScalarGridSpec(
        num_scalar_prefetch=3,
        grid=(bd, n_full),
        in_specs=[pl.BlockSpec((1, rows, w), lambda b, j, *_: (b, 0, 0)),
                  pl.BlockSpec((1, tq, w), lambda b, j, *_: (b, 0, 0)),
                  pl.BlockSpec((1, tq, w), lambda b, j, *_: (b, 0, 0))]
        + [page_spec(i) for i in range(ppb)] + [page_spec(i) for i in range(ppb)],
        out_specs=pl.BlockSpec((1, rows, w), lambda b, j, *_: (b, 0, 0)),
        scratch_shapes=[pltpu.VMEM((N_HEADS * rows, hd), BF16),
                        pltpu.VMEM((N_HEADS * rows, MOBA_BLOCK * N_HEADS), F32),
                        pltpu.VMEM((N_HEADS * rows, 1), F32), pltpu.VMEM((N_HEADS * rows, 1), F32),
                        pltpu.VMEM((N_HEADS * rows, hd), F32)])
    kern = functools.partial(_sample_attn_kernel, tq=tq, past=n_pages * page, n_full=n_full, scale=hd ** -0.5)
    return pl.pallas_call(
        kern, grid_spec=grid_spec,
        out_shape=jax.ShapeDtypeStruct((bd, rows, w), F32),
        compiler_params=_params(("parallel", "arbitrary")),
        name="sample_attention",
    )(page_table.reshape(-1), sel_bits, slopes, q_pad, k_new, v_new,
      *([cache_k5] * ppb), *([cache_v5] * ppb))


def _moba_sample(qs, ks, vs, cache_k5, cache_v5, layer, page_table, slopes):
    bd, tq, w = qs.shape
    n_pages = page_table.shape[1]
    assert (n_pages * PAGE_SIZE) % MOBA_BLOCK == 0 and n_pages > 0 and tq <= 8
    n_full = n_pages * PAGE_SIZE // MOBA_BLOCK
    kmean = _cache_block_means(cache_k5, layer, page_table)
    q_pad = jnp.pad(qs, ((0, 0), (0, 8 - tq), (0, 0)))
    top = _sample_topk(q_pad, kmean).reshape(bd, N_HEADS, 8, LANES)[:, :, :tq, :MOBA_TOPK]
    hit = jnp.any(top[..., None] == jnp.arange(n_full, dtype=jnp.int32), axis=3)
    sel_bits = jnp.sum(jnp.where(hit, 1 << jnp.arange(tq, dtype=jnp.int32)[:, None], 0), axis=2)
    out = _sample_attention(q_pad, ks, vs, cache_k5, cache_v5, layer, page_table,
                            sel_bits.reshape(-1).astype(jnp.int32), slopes)
    return out[:, :tq, :]


def _ssd_kernel(x_ref, b_ref, c_ref, wx_ref, wb_ref, wc_ref, bx_ref, bb_ref, bc_ref,
                ix_ref, ib_ref, ic_ref, dt_ref, dtt_ref, dtb_row_ref, dtb_col_ref,
                alog_row_ref, alog_col_ref, dskip_ref, z_ref, gssm_ref, s0_ref,
                y_ref, sfin_ref, carry_ref, cbuf_ref, ubuf_ref, *, t_valid, n_groups, d_inner):
    L = SSD_CHUNK
    gw = x_ref.shape[2]
    n_state = b_ref.shape[2]
    hpg = gw // SSM_HEADDIM
    cw = gw + 2 * n_state
    c = pl.program_id(1)
    g = pl.program_id(2)
    rows_g = pl.ds(pl.multiple_of(g * gw, gw), gw)

    @pl.when(c == 0)
    def _():
        carry_ref[g] = jnp.zeros((8, cw), F32)
        carry_ref[g, 5:8, 0:gw] = ix_ref[0]
        carry_ref[g, 5:8, gw:gw + n_state] = ib_ref[0]
        carry_ref[g, 5:8, gw + n_state:cw] = ic_ref[0]
        sfin_ref[0, rows_g, :] = s0_ref[0, rows_g, :]

    cbuf_ref[0:8, :] = carry_ref[g]
    cbuf_ref[8:8 + L, 0:gw] = x_ref[0]
    cbuf_ref[8:8 + L, gw:gw + n_state] = b_ref[0]
    cbuf_ref[8:8 + L, gw + n_state:cw] = c_ref[0]
    carry_ref[g] = cbuf_ref[L:L + 8, :]
    w = jnp.concatenate([wx_ref[...], wb_ref[...], wc_ref[...]], axis=1)
    bias = jnp.concatenate([bx_ref[...], bb_ref[...], bc_ref[...]], axis=1)
    conv = bias + w[0:1] * cbuf_ref[5:5 + L, :]
    for i in range(1, 4):
        conv = conv + w[i:i + 1] * cbuf_ref[5 + i:5 + i + L, :]
    act = _silu(conv)
    xs = act[:, 0:gw]
    bm = act[:, gw:gw + n_state].astype(BF16)
    cm = act[:, gw + n_state:cw].astype(BF16)

    dt = _softplus(dt_ref[0, 0] + dtb_row_ref[0])
    dtt = _softplus(dtt_ref[0, 0] + dtb_col_ref[0])
    if t_valid < L:
        dt = jnp.where(lax.broadcasted_iota(jnp.int32, dt.shape, 0) < t_valid, dt, 0.0)
        dtt = jnp.where(lax.broadcasted_iota(jnp.int32, dtt.shape, 1) < t_valid, dtt, 0.0)
    adt = dt * (-jnp.exp(alog_row_ref[0]))
    adtt = dtt * (-jnp.exp(alog_col_ref[0]))

    row = lax.broadcasted_iota(jnp.int32, (L, L), 0)
    col = lax.broadcasted_iota(jnp.int32, (L, L), 1)
    causal = col <= row
    tri_l = jnp.where(causal, 1.0, 0.0).astype(BF16)
    tri_u = jnp.where(row <= col, 1.0, 0.0).astype(BF16)
    acs_col = _dot_exact_lhs(tri_l, adt)
    acs_row = _dot_exact_rhs(adtt, tri_u)

    er = lax.broadcasted_iota(jnp.int32, (LANES, gw), 0)
    ec = lax.broadcasted_iota(jnp.int32, (LANES, gw), 1)
    expand = jnp.where(er == ec // SSM_HEADDIM, 1.0, 0.0).astype(BF16)
    eacs = jnp.exp(acs_col)
    dte = jnp.exp(acs_col[L - 1:L, :] - acs_col)
    st_hi, st_lo = _split2(jnp.concatenate([dt, eacs, dte], axis=0))
    stacked = _dot(st_hi, expand) + _dot(st_lo, expand)
    dt_x = stacked[0:L]
    eacs_x = stacked[L:2 * L]
    dte_x = stacked[2 * L:3 * L]

    xdt = xs * dt_x
    xdt_b = xdt.astype(BF16)
    cb = _dot_nt(cm, bm)
    lane = lax.broadcasted_iota(jnp.int32, (L, LANES), 1)
    pairs = []
    for i in range(hpg // 2):
        xp = xdt_b[:, i * LANES:(i + 1) * LANES]
        res = []
        for j in (2 * i, 2 * i + 1):
            diff = acs_col[:, j:j + 1] - acs_row[j:j + 1, :]
            decay = jnp.exp(jnp.where(causal, diff, NEG_INF))
            res.append(_dot((cb * decay).astype(BF16), xp))
        pairs.append(jnp.where(lane < SSM_HEADDIM, res[0], res[1]))
    y_diag = jnp.concatenate(pairs, axis=1)

    s_prev = sfin_ref[0, rows_g, :]
    y_off = _dot_nt(cm, s_prev.astype(BF16)) * eacs_x
    y = y_diag + y_off + dskip_ref[...] * xs
    ubuf_ref[g] = y * _silu(z_ref[0])

    xw_t = (xdt * dte_x).T.astype(BF16)
    s_chunk = _dot(xw_t, bm)
    last = jnp.broadcast_to(acs_row[:, L - 1:L], (hpg, n_state))
    last = jnp.concatenate([last, jnp.zeros((LANES - hpg, n_state), F32)], axis=0)
    tr = lax.broadcasted_iota(jnp.int32, (gw, LANES), 0)
    tc = lax.broadcasted_iota(jnp.int32, (gw, LANES), 1)
    expand_t = jnp.where(tc == tr // SSM_HEADDIM, 1.0, 0.0).astype(BF16)
    chunk_decay = jnp.exp(_dot_exact_lhs(expand_t, last))
    sfin_ref[0, rows_g, :] = chunk_decay * s_prev + s_chunk

    @pl.when(g == n_groups - 1)
    def _():
        ss = jnp.zeros((L, 1), F32)
        for gi in range(n_groups):
            u = ubuf_ref[gi]
            ss = ss + jnp.sum(u * u, axis=1, keepdims=True)
        r = lax.rsqrt(ss * (1.0 / d_inner) + NORM_EPS)
        for gi in range(n_groups):
            yn = (ubuf_ref[gi] * r) * gssm_ref[:, gi * gw:(gi + 1) * gw]
            y_ref[0, :, gi * gw:(gi + 1) * gw] = yn.astype(y_ref.dtype)


def _ssd(xbc3, z3, dt_g, dtt_g, conv_init, ssm_init, conv_w, conv_b, dt_bias, a_log, d_skip, g_ssm,
         t_valid):
    bm, t, conv_dim = xbc3.shape
    d_inner = z3.shape[2]
    G = SSM_GROUPS
    n_state = SSM_STATE
    gw = d_inner // G
    hpg = gw // SSM_HEADDIM
    assert hpg % 2 == 0 and hpg <= 8 and n_state == LANES and 2 * SSM_HEADDIM == LANES
    L = SSD_CHUNK
    assert t % L == 0 and (t_valid == L or t == L)
    nc = t // L
    cw = gw + 2 * n_state
    xb = d_inner // n_state
    cbk = xb + G
    pad = lambda a: jnp.pad(a, ((0, 0), (0, 0), (0, LANES - a.shape[2])))
    dtb_row = pad(dt_bias.reshape(G, 1, hpg))
    dtb_col = dt_bias.reshape(G, hpg, 1)
    alog_row = pad(a_log.reshape(G, 1, hpg))
    alog_col = a_log.reshape(G, hpg, 1)
    dskip_x = jnp.repeat(d_skip, SSM_HEADDIM).reshape(1, d_inner)
    conv_b2 = conv_b.reshape(1, conv_dim)
    kern = functools.partial(_ssd_kernel, t_valid=t_valid, n_groups=G, d_inner=d_inner)
    in_specs = [
        pl.BlockSpec((1, L, gw), lambda b, c, g: (b, c, g)),
        pl.BlockSpec((1, L, n_state), lambda b, c, g: (b, c, xb + g)),
        pl.BlockSpec((1, L, n_state), lambda b, c, g: (b, c, cbk + g)),
        pl.BlockSpec((4, gw), lambda b, c, g: (0, g)),
        pl.BlockSpec((4, n_state), lambda b, c, g: (0, xb + g)),
        pl.BlockSpec((4, n_state), lambda b, c, g: (0, cbk + g)),
        pl.BlockSpec((1, gw), lambda b, c, g: (0, g)),
        pl.BlockSpec((1, n_state), lambda b, c, g: (0, xb + g)),
        pl.BlockSpec((1, n_state), lambda b, c, g: (0, cbk + g)),
        pl.BlockSpec((1, 3, gw), lambda b, c, g: (b, 0, g)),
        pl.BlockSpec((1, 3, n_state), lambda b, c, g: (b, 0, xb + g)),
        pl.BlockSpec((1, 3, n_state), lambda b, c, g: (b, 0, cbk + g)),
        pl.BlockSpec((1, 1, L, LANES), lambda b, c, g: (b, g, c, 0)),
        pl.BlockSpec((1, 1, hpg, L), lambda b, c, g: (b, g, 0, c)),
        pl.BlockSpec((1, 1, LANES), lambda b, c, g: (g, 0, 0)),
        pl.BlockSpec((1, hpg, 1), lambda b, c, g: (g, 0, 0)),
        pl.BlockSpec((1, 1, LANES), lambda b, c, g: (g, 0, 0)),
        pl.BlockSpec((1, hpg, 1), lambda b, c, g: (g, 0, 0)),
        pl.BlockSpec((1, gw), lambda b, c, g: (0, g)),
        pl.BlockSpec((1, L, gw), lambda b, c, g: (b, c, g)),
        pl.BlockSpec((1, d_inner), lambda b, c, g: (0, 0)),
        pl.BlockSpec((1, d_inner, n_state), lambda b, c, g: (b, 0, 0)),
    ]
    out_specs = [pl.BlockSpec((1, L, d_inner), lambda b, c, g: (b, c, 0)),
                 pl.BlockSpec((1, d_inner, n_state), lambda b, c, g: (b, 0, 0))]
    return pl.pallas_call(
        kern, grid=(bm, nc, G), in_specs=in_specs, out_specs=out_specs,
        out_shape=[jax.ShapeDtypeStruct((bm, t, d_inner), BF16),
                   jax.ShapeDtypeStruct((bm, d_inner, n_state), F32)],
        scratch_shapes=[pltpu.VMEM((G, 8, cw), F32), pltpu.VMEM((8 + L, cw), F32),
                        pltpu.VMEM((G, L, gw), F32)],
        compiler_params=_params(("parallel", "arbitrary", "arbitrary")),
        name="ssd",
    )(xbc3, xbc3, xbc3, conv_w, conv_w, conv_w, conv_b2, conv_b2, conv_b2,
      conv_init, conv_init, conv_init, dt_g, dtt_g, dtb_row, dtb_col, alog_row, alog_col,
      dskip_x, z3, g_ssm.reshape(1, d_inner), ssm_init)


def _merge_kernel(a_ref, y_ref, wa_ref, ws_ref, ga_ref, gs_ref, ba_ref, bs_ref, o_ref):
    pa = _dot(a_ref[...], wa_ref[...])
    ps = _dot(y_ref[...], ws_ref[...])
    o = _sigmoid(ga_ref[...] + ba_ref[...]) * pa + _sigmoid(gs_ref[...] + bs_ref[...]) * ps
    o_ref[...] = o.astype(o_ref.dtype)


def _merge(attn, y, w_a, w_s, ga, gs, b_gate, tm, tn):
    m, ka = attn.shape
    ks = y.shape[1]
    n = w_a.shape[1]
    tm = min(tm, m)
    ba = b_gate[:n].reshape(1, n)
    bs = b_gate[n:].reshape(1, n)
    return pl.pallas_call(
        _merge_kernel, grid=(m // tm, n // tn),
        in_specs=[pl.BlockSpec((tm, ka), lambda i, j: (i, 0)),
                  pl.BlockSpec((tm, ks), lambda i, j: (i, 0)),
                  pl.BlockSpec((ka, tn), lambda i, j: (0, j)),
                  pl.BlockSpec((ks, tn), lambda i, j: (0, j)),
                  pl.BlockSpec((tm, tn), lambda i, j: (i, j)),
                  pl.BlockSpec((tm, tn), lambda i, j: (i, j)),
                  pl.BlockSpec((1, tn), lambda i, j: (0, j)),
                  pl.BlockSpec((1, tn), lambda i, j: (0, j))],
        out_specs=pl.BlockSpec((tm, tn), lambda i, j: (i, j)),
        out_shape=jax.ShapeDtypeStruct((m, n), BF16),
        compiler_params=_params(("parallel", "parallel")),
        name="merge",
    )(attn, y, w_a, w_s, ga, gs, ba, bs)


def _outproj_route_kernel(mg_ref, wout_ref, x_ref, g1_ref, sc2_ref, sh2_ref, gffn_ref,
                          wgh_ref, wgl_ref, bg_ref, wrh_ref, wrl_ref, br_ref,
                          x1_ref, h2_ref, ids_ref, ew_ref, cnt_ref, carry_ref):
    i = pl.program_id(0)

    @pl.when(i == 0)
    def _():
        carry_ref[...] = jnp.zeros_like(carry_ref)

    x1 = x_ref[...] + g1_ref[0, 0] * _dot(mg_ref[...], wout_ref[...])
    x1_ref[...] = x1
    xn = x1 * lax.rsqrt(jnp.mean(x1 * x1, axis=-1, keepdims=True) + NORM_EPS)
    h2 = (xn * gffn_ref[...]) * (1.0 + sc2_ref[0, 0]) + sh2_ref[0, 0]
    h2_ref[...] = h2

    hh, hl = _split2(h2)
    lg = _dot(hh, wgh_ref[...]) + _dot(hh, wgl_ref[...]) + _dot(hl, wgh_ref[...]) + bg_ref[...]
    le = _dot(hh, wrh_ref[...]) + _dot(hh, wrl_ref[...]) + _dot(hl, wrh_ref[...]) + br_ref[...]
    tm = lg.shape[0]
    lane_i = lax.broadcasted_iota(jnp.int32, (tm, LANES), 1)
    lane = lane_i.astype(F32)
    in_g = lane_i < N_EXPERT_GROUPS
    lgm = jnp.where(in_g, lg, NEG_INF)
    mg = jnp.max(lgm, axis=1, keepdims=True)
    gtop = jnp.min(jnp.where(lgm == mg, lane, float(LANES)), axis=1, keepdims=True)
    gw = 1.0 / jnp.sum(jnp.where(in_g, jnp.exp(lg - mg), 0.0), axis=1, keepdims=True)
    grp = (lane_i // EXPERTS_PER_GROUP).astype(F32)
    lem = jnp.where((lane_i < N_EXPERTS) & (grp == gtop), le, NEG_INF)
    e1 = jnp.max(lem, axis=1, keepdims=True)
    i1 = jnp.min(jnp.where(lem == e1, lane, float(LANES)), axis=1, keepdims=True)
    lem2 = jnp.where(lane == i1, NEG_INF, lem)
    e2 = jnp.max(lem2, axis=1, keepdims=True)
    i2 = jnp.min(jnp.where(lem2 == e2, lane, float(LANES)), axis=1, keepdims=True)
    t2 = jnp.exp(e2 - e1)
    den = 1.0 + t2
    w1 = (1.0 / den) * gw
    w2 = (t2 / den) * gw

    hit1 = lane == i1
    hit2 = lane == i2
    onehot = jnp.where(hit1 | hit2, 1.0, 0.0)
    rr = lax.broadcasted_iota(jnp.int32, (tm, tm), 0)
    cc = lax.broadcasted_iota(jnp.int32, (tm, tm), 1)
    strict = jnp.where(cc < rr, 1.0, 0.0).astype(BF16)
    cum = _dot(strict, onehot.astype(BF16)) + carry_ref[...]
    r1 = jnp.sum(jnp.where(hit1, cum, 0.0), axis=1, keepdims=True)
    r2 = jnp.sum(jnp.where(hit2, cum, 0.0), axis=1, keepdims=True)
    carry_ref[...] = carry_ref[...] + jnp.sum(onehot, axis=0, keepdims=True)
    cnt_ref[...] = carry_ref[...]
    packed = jnp.where(lane_i == 0, i1, jnp.where(lane_i == 1, i2,
                       jnp.where(lane_i == 2, r1, jnp.where(lane_i == 3, r2, 0.0))))
    ids_ref[...] = packed.astype(jnp.int32)
    ew_ref[...] = jnp.where(lane_i == 0, w1, jnp.where(lane_i == 1, w2, 0.0))


def _outproj_route(merged, w_out, x2, mod4, t_seq, g_ffn, w_group, b_group, w_router, b_router, tm):
    m, d = x2.shape
    r = mod4.shape[2]
    assert t_seq % tm == 0 and (r == 1 or tm == t_seq)
    per_seq = t_seq // tm

    def pad_cols(w):
        return jnp.pad(w, ((0, 0), (0, LANES - w.shape[1])))

    wg = pad_cols(w_group)
    wr = pad_cols(w_router)
    wgh, wgl = _split2(wg)
    wrh, wrl = _split2(wr)
    bg = pad_cols(b_group.reshape(1, -1))
    br = pad_cols(b_router.reshape(1, -1))
    row_spec = pl.BlockSpec((tm, d), lambda i: (i, 0))
    full = lambda shape: pl.BlockSpec(shape, lambda i: tuple(0 for _ in shape))
    mod_spec = lambda idx: pl.BlockSpec((1, 1, r, d), lambda i: (idx, i // per_seq, 0, 0))
    lane_spec = pl.BlockSpec((tm, LANES), lambda i: (i, 0))
    return pl.pallas_call(
        _outproj_route_kernel, grid=(m // tm,),
        in_specs=[row_spec, full((d, d)), row_spec, mod_spec(2), mod_spec(4), mod_spec(3), full((1, d)),
                  full((d, LANES)), full((d, LANES)), full((1, LANES)),
                  full((d, LANES)), full((d, LANES)), full((1, LANES))],
        out_specs=[row_spec, row_spec, lane_spec, lane_spec, full((1, LANES))],
        out_shape=[jax.ShapeDtypeStruct((m, d), F32), jax.ShapeDtypeStruct((m, d), F32),
                   jax.ShapeDtypeStruct((m, LANES), jnp.int32), jax.ShapeDtypeStruct((m, LANES), F32),
                   jax.ShapeDtypeStruct((1, LANES), F32)],
        scratch_shapes=[pltpu.VMEM((1, LANES), F32)],
        compiler_params=_params(("arbitrary",)),
        name="outproj_route",
    )(merged, w_out, x2, mod4, mod4, mod4, g_ffn.reshape(1, d), wgh, wgl, bg, wrh, wrl, br)


DMA_UNROLL = 8


def _row_copy(src_hbm, row, dst, slot, sem):
    return pltpu.make_async_copy(src_hbm.at[pl.ds(row, 1), :], dst.at[pl.ds(slot, 1), :], sem)


def _expert_kernel(bexp_ref, nb_ref, tok_ref, h2_hbm, wg_ref, wu_ref, wd_ref, o_ref, xbuf, sems):
    blk = pl.program_id(0)
    n_used = nb_ref[0]

    def gather(block, buf):
        base = block * MOE_BLOCK

        def issue(r, carry):
            _row_copy(h2_hbm, tok_ref[base + r], xbuf.at[buf], r, sems.at[buf]).start()
            return carry

        lax.fori_loop(0, MOE_BLOCK, issue, 0, unroll=DMA_UNROLL)

    @pl.when(blk == 0)
    def _():
        gather(0, 0)

    @pl.when(blk < n_used)
    def _():
        cur = blk % 2

        @pl.when(blk + 1 < n_used)
        def _():
            gather(blk + 1, 1 - cur)

        def drain(r, carry):
            _row_copy(h2_hbm, 0, xbuf.at[cur], r, sems.at[cur]).wait()
            return carry

        lax.fori_loop(0, MOE_BLOCK, drain, 0, unroll=DMA_UNROLL)
        x = xbuf[cur].astype(BF16)
        gate = _dot(x, wg_ref[0])
        up = _dot(x, wu_ref[0])
        act = (_silu(gate) * up).astype(BF16)
        o_ref[...] = _dot(act, wd_ref[0])

    @pl.when(blk >= n_used)
    def _():
        o_ref[...] = jnp.zeros_like(o_ref)


def _experts(h2, slot_tok, block_expert, n_used, w_gate_b, w_up_b, w_down_b):
    t, d = h2.shape
    n_slots = slot_tok.shape[0]
    n_blocks = n_slots // MOE_BLOCK
    hid = w_up_b.shape[2]
    grid_spec = pltpu.PrefetchScalarGridSpec(
        num_scalar_prefetch=3,
        grid=(n_blocks,),
        in_specs=[pl.BlockSpec(memory_space=pl.ANY),
                  pl.BlockSpec((1, d, hid), lambda i, be, nb, tok: (be[i], 0, 0)),
                  pl.BlockSpec((1, d, hid), lambda i, be, nb, tok: (be[i], 0, 0)),
                  pl.BlockSpec((1, hid, d), lambda i, be, nb, tok: (be[i], 0, 0))],
        out_specs=pl.BlockSpec((MOE_BLOCK, d), lambda i, be, nb, tok: (i, 0)),
        scratch_shapes=[pltpu.VMEM((2, MOE_BLOCK, d), F32), pltpu.SemaphoreType.DMA((2,))])
    return pl.pallas_call(
        _expert_kernel, grid_spec=grid_spec,
        out_shape=jax.ShapeDtypeStruct((n_slots, d), F32),
        compiler_params=_params(("arbitrary",)),
        name="experts",
    )(block_expert, n_used, slot_tok, h2, w_gate_b, w_up_b, w_down_b)


def _combine_kernel(dest_ref, yb_hbm, ew_ref, x1_ref, g2_ref, gfin_ref, o_ref, buf, sem):
    i = pl.program_id(0)
    tm = x1_ref.shape[0]
    n_rows = buf.shape[0]
    base = i * n_rows

    def issue(r, carry):
        _row_copy(yb_hbm, dest_ref[base + r], buf, r, sem).start()
        return carry

    lax.fori_loop(0, n_rows, issue, 0, unroll=DMA_UNROLL)

    def drain(r, carry):
        _row_copy(yb_hbm, 0, buf, r, sem).wait()
        return carry

    lax.fori_loop(0, n_rows, drain, 0, unroll=DMA_UNROLL)
    w1 = ew_ref[:, 0:1]
    w2 = ew_ref[:, 1:2]
    moe = buf[0:tm, :] * w1 + buf[tm:2 * tm, :] * w2
    x2 = x1_ref[...] + g2_ref[0, 0] * moe
    xn = x2 * lax.rsqrt(jnp.mean(x2 * x2, axis=-1, keepdims=True) + NORM_EPS)
    o_ref[...] = xn * gfin_ref[...]


def _combine(yb, dest_tiles, ew, x1, mod4, t_seq, g_final, tm):
    m, d = x1.shape
    r = mod4.shape[2]
    assert t_seq % tm == 0 and (r == 1 or tm == t_seq)
    per_seq = t_seq // tm
    grid_spec = pltpu.PrefetchScalarGridSpec(
        num_scalar_prefetch=1,
        grid=(m // tm,),
        in_specs=[pl.BlockSpec(memory_space=pl.ANY),
                  pl.BlockSpec((tm, LANES), lambda i, de: (i, 0)),
                  pl.BlockSpec((tm, d), lambda i, de: (i, 0)),
                  pl.BlockSpec((1, 1, r, d), lambda i, de: (5, i // per_seq, 0, 0)),
                  pl.BlockSpec((1, d), lambda i, de: (0, 0))],
        out_specs=pl.BlockSpec((tm, d), lambda i, de: (i, 0)),
        scratch_shapes=[pltpu.VMEM((2 * tm, d), F32), pltpu.SemaphoreType.DMA])
    return pl.pallas_call(
        _combine_kernel, grid_spec=grid_spec,
        out_shape=jax.ShapeDtypeStruct((m, d), F32),
        compiler_params=_params(("arbitrary",)),
        name="combine",
    )(dest_tiles, yb, ew, x1, mod4, g_final.reshape(1, d))


def _moe_and_final_norm(merged, x2, mod4, t_seq, lw, g_final, tm_route, tm_comb):
    m, d = x2.shape
    x1, h2, ids, ew, cnt = _outproj_route(merged, lw["w_out_b"], x2, mod4, t_seq, lw["g_ffn"],
                                          lw["w_group"], lw["b_group"], lw["w_router"], lw["b_router"],
                                          tm_route)
    e_id = ids[:, 0:2]
    rank = ids[:, 2:4]
    counts = cnt[0, :N_EXPERTS].astype(jnp.int32)
    padded = (counts + MOE_BLOCK - 1) // MOE_BLOCK * MOE_BLOCK
    pad_end = jnp.cumsum(padded)
    pad_start = pad_end - padded
    experts = jnp.arange(N_EXPERTS, dtype=jnp.int32)
    dest = jnp.sum(jnp.where(e_id[..., None] == experts, pad_start, 0), axis=-1) + rank
    n_assign = 2 * m
    n_blocks = (n_assign + N_EXPERTS * (MOE_BLOCK - 1) + MOE_BLOCK - 1) // MOE_BLOCK
    n_slots = n_blocks * MOE_BLOCK
    tok = jnp.broadcast_to(jnp.arange(m, dtype=jnp.int32)[:, None], (m, 2))
    slot_tok = jnp.zeros((n_slots,), jnp.int32).at[dest.reshape(-1)].set(tok.reshape(-1))
    block_start = jnp.arange(n_blocks, dtype=jnp.int32) * MOE_BLOCK
    block_expert = jnp.minimum(jnp.sum((pad_end[None, :] <= block_start[:, None]).astype(jnp.int32), axis=1),
                               N_EXPERTS - 1)
    n_used = (pad_end[-1:] // MOE_BLOCK).astype(jnp.int32)
    yb = _experts(h2, slot_tok, block_expert, n_used, lw["w_gate_e_b"], lw["w_up_b"], lw["w_down_b"])
    dest_tiles = dest.reshape(m // tm_comb, tm_comb, 2).transpose(0, 2, 1).reshape(-1).astype(jnp.int32)
    return _combine(yb, dest_tiles, ew, x1, mod4, t_seq, g_final, tm_comb)


def _input_projections(h1, lw, tm):
    proj = {name: _matmul(h1, lw["w_" + name], tm, 512) for name in ("q", "k", "v", "z", "xbc", "ga", "gs")}
    proj["dt"] = _matmul(h1, lw["w_dt"], tm, LANES)
    return proj


def _dt_layouts(dt_raw, bm, t, t_pad):
    hpg = HEADS_PER_GROUP
    d = dt_raw[:, :SSM_GROUPS * hpg].reshape(bm, t, SSM_GROUPS, hpg)
    d = jnp.pad(d, ((0, 0), (0, t_pad - t), (0, 0), (0, 0)))
    dt_g = jnp.pad(d.transpose(0, 2, 1, 3), ((0, 0), (0, 0), (0, 0), (0, LANES - hpg)))
    dtt_g = d.transpose(0, 2, 3, 1)
    return dt_g, dtt_g


def kernel(x_prompt, x_sample, cache_k, cache_v, state_ssm, state_conv, page_table, c_prompt, c_sample,
           w_ada, b_ada, g_mix, w_in, b_gate, conv_w, conv_b, dt_bias, a_log, d_skip, g_ssm,
           w_attn_br, w_ssm_br, w_out, g_ffn, w_group, b_group, w_router, b_router,
           w_up, w_gate_e, w_down, g_final):
    depth = w_ada.shape[0]
    assert depth == 1
    bp, t, d = x_prompt.shape
    bd, tq, _ = x_sample.shape
    att_w = N_HEADS * (d // N_HEADS)
    d_inner = g_ssm.shape[1]
    conv_dim = conv_w.shape[2]
    n_ssm_heads = dt_bias.shape[1]
    assert n_ssm_heads == SSM_GROUPS * HEADS_PER_GROUP and d_inner == n_ssm_heads * SSM_HEADDIM
    slopes = 2.0 ** (-8.0 * jnp.arange(1, N_HEADS + 1, dtype=F32) / N_HEADS)

    l = 0
    w_in_b = w_in[l].astype(BF16)
    cuts = [0, att_w, 2 * att_w, 3 * att_w, 3 * att_w + d_inner, 3 * att_w + d_inner + conv_dim,
            3 * att_w + d_inner + conv_dim + n_ssm_heads]
    cuts += [cuts[-1] + d, cuts[-1] + 2 * d]
    names = ("q", "k", "v", "z", "xbc", "dt", "ga", "gs")
    lw = {"w_" + n: w_in_b[:, cuts[i]:cuts[i + 1]] for i, n in enumerate(names)}
    lw["w_dt"] = jnp.pad(lw["w_dt"], ((0, 0), (0, LANES - n_ssm_heads)))
    lw.update(w_attn_b=w_attn_br[l].astype(BF16), w_ssm_b=w_ssm_br[l].astype(BF16),
              w_out_b=w_out[l].astype(BF16), g_ffn=g_ffn[l], w_group=w_group[l], b_group=b_group[l],
              w_router=w_router[l], b_router=b_router[l], w_up_b=w_up[l].astype(BF16),
              w_gate_e_b=w_gate_e[l].astype(BF16), w_down_b=w_down[l].astype(BF16))

    mod = _ada_mod(jnp.concatenate([c_prompt, c_sample], axis=0), w_ada[l], b_ada[l])
    mod = mod.reshape(bp + bd, 6, d)
    mod_p = mod[:bp].transpose(1, 0, 2).reshape(6, bp, 1, d)
    mod_s = jnp.repeat(mod[bp:].transpose(1, 0, 2), tq, axis=1).reshape(6, 1, bd * tq, d)

    ssd_args = (conv_w[l], conv_b[l], dt_bias[l], a_log[l], d_skip[l], g_ssm[l])

    mp = bp * t
    h1 = _modnorm(x_prompt, g_mix[l], mod_p, 1, 0, 512)
    pj = _input_projections(h1, lw, 1024)
    q3, k3, v3 = (pj[n].reshape(bp, t, att_w) for n in ("q", "k", "v"))
    attn = _moba_prompt(q3, k3, v3, slopes).reshape(mp, att_w)
    xbc3 = pj["xbc"].reshape(bp, t, conv_dim)
    dt_g, dtt_g = _dt_layouts(pj["dt"], bp, t, t)
    y, ssm_p = _ssd(xbc3, pj["z"].reshape(bp, t, d_inner), dt_g, dtt_g,
                    jnp.zeros((bp, 3, conv_dim), F32), jnp.zeros((bp, d_inner, SSM_STATE), F32),
                    *ssd_args, t_valid=SSD_CHUNK)
    merged = _merge(attn, y.reshape(mp, d_inner), lw["w_attn_b"], lw["w_ssm_b"], pj["ga"], pj["gs"],
                    b_gate[l], 512, 512)
    y_prompt = _moe_and_final_norm(merged, x_prompt.reshape(mp, d), mod_p, t, lw, g_final, 256, 128)
    y_prompt = y_prompt.reshape(bp, t, d)
    k_p = k3.reshape(1, bp, t, N_HEADS, att_w // N_HEADS)
    v_p = v3.reshape(1, bp, t, N_HEADS, att_w // N_HEADS)
    conv_p = xbc3[:, t - 3:, :][None]
    ssm_p = ssm_p.reshape(1, bp, n_ssm_heads, SSM_HEADDIM, SSM_STATE)

    ms = bd * tq
    xs3 = x_sample.reshape(1, ms, d)
    h1s = _modnorm(xs3, g_mix[l], mod_s, 1, 0, ms)
    pjs = _input_projections(h1s, lw, ms)
    qs, ks, vs = (pjs[n].reshape(bd, tq, att_w) for n in ("q", "k", "v"))
    attn_s = _moba_sample(qs, ks, vs, cache_k, cache_v, l, page_table, slopes)
    attn_s = attn_s.reshape(ms, att_w).astype(BF16)
    xbc_s = pjs["xbc"].reshape(bd, tq, conv_dim)
    pad_t = lambda a: jnp.pad(a, ((0, 0), (0, SSD_CHUNK - tq), (0, 0)))
    dt_gs, dtt_gs = _dt_layouts(pjs["dt"], bd, tq, SSD_CHUNK)
    y_s, ssm_s = _ssd(pad_t(xbc_s), pad_t(pjs["z"].reshape(bd, tq, d_inner)), dt_gs, dtt_gs,
                      state_conv[l], state_ssm[l].reshape(bd, d_inner, SSM_STATE), *ssd_args, t_valid=tq)
    y_s = y_s[:, :tq, :].reshape(ms, d_inner)
    merged_s = _merge(attn_s, y_s, lw["w_attn_b"], lw["w_ssm_b"], pjs["ga"], pjs["gs"], b_gate[l], ms, 512)
    y_sample = _moe_and_final_norm(merged_s, x_sample.reshape(ms, d), mod_s, ms, lw, g_final, ms, ms)
    y_sample = y_sample.reshape(bd, tq, d)
    k_s = ks.reshape(1, bd, tq, N_HEADS, att_w // N_HEADS)
    v_s = vs.reshape(1, bd, tq, N_HEADS, att_w // N_HEADS)
    conv_s = jnp.concatenate([state_conv[l], xbc_s], axis=1)[:, tq:, :][None]
    ssm_s = ssm_s.reshape(1, bd, n_ssm_heads, SSM_HEADDIM, SSM_STATE)

    return (y_prompt, y_sample, k_p, v_p, k_s, v_s, ssm_p, ssm_s, conv_p, conv_s)
```

```python
import functools

import jax
import jax.numpy as jnp
from jax import lax
from jax.experimental import pallas as pl
from jax.experimental.pallas import tpu as pltpu

F32 = jnp.float32
BF16 = jnp.bfloat16
NEG_INF = float("-inf")

PAGE_SIZE = 128
N_HEADS = 16
MOBA_BLOCK = 256
MOBA_TOPK = 3
SSM_HEADDIM = 64
SSM_GROUPS = 8
SSM_STATE = 128
HEADS_PER_GROUP = 8
SSD_CHUNK = 128
N_EXPERT_GROUPS = 4
EXPERTS_PER_GROUP = 8
N_EXPERTS = N_EXPERT_GROUPS * EXPERTS_PER_GROUP
MOE_BLOCK = 128
NORM_EPS = 1e-6

LANES = 128
VMEM_LIMIT_MB = 56


def _params(semantics, vmem_mb=VMEM_LIMIT_MB):
    return pltpu.CompilerParams(dimension_semantics=semantics,
                                vmem_limit_bytes=vmem_mb * 1024 * 1024)


def _sigmoid(x):
    return 0.5 * jnp.tanh(0.5 * x) + 0.5


def _silu(x):
    return x * _sigmoid(x)


def _softplus(x):
    return jnp.maximum(x, 0.0) + jnp.log1p(jnp.exp(-jnp.abs(x)))


def _dot(a, b):
    return jnp.dot(a, b, preferred_element_type=F32)


def _dot_nt(a, b):
    return lax.dot_general(a, b, (((1,), (1,)), ((), ())), preferred_element_type=F32)


def _split2(x):
    hi = x.astype(BF16)
    lo = (x - hi.astype(F32)).astype(BF16)
    return hi, lo


def _split3(x):
    p1 = x.astype(BF16)
    r1 = x - p1.astype(F32)
    p2 = r1.astype(BF16)
    p3 = (r1 - p2.astype(F32)).astype(BF16)
    return p1, p2, p3


def _dot_exact_rhs(x, e):
    p1, p2, p3 = _split3(x)
    return _dot(p1, e) + _dot(p2, e) + _dot(p3, e)


def _dot_exact_lhs(e, x):
    p1, p2, p3 = _split3(x)
    return _dot(e, p1) + _dot(e, p2) + _dot(e, p3)


def _dot_hi_nt(a, b):
    ah, al = _split2(a)
    bh, bl = _split2(b)
    return _dot_nt(ah, bh) + _dot_nt(ah, bl) + _dot_nt(al, bh)


def _ada_kernel(c_ref, w_ref, b_ref, o_ref):
    a = _silu(c_ref[...]).astype(BF16)
    o_ref[...] = _dot(a, w_ref[...].astype(BF16)) + b_ref[...]


def _ada_mod(c, w_ada, b_ada):
    mc, d = c.shape
    n = w_ada.shape[1]
    tn = 1024
    return pl.pallas_call(
        _ada_kernel,
        grid=(n // tn,),
        in_specs=[pl.BlockSpec((mc, d), lambda j: (0, 0)),
                  pl.BlockSpec((d, tn), lambda j: (0, j)),
                  pl.BlockSpec((1, tn), lambda j: (0, j))],
        out_specs=pl.BlockSpec((mc, tn), lambda j: (0, j)),
        out_shape=jax.ShapeDtypeStruct((mc, n), F32),
        compiler_params=_params(("parallel",)),
        name="ada_mod",
    )(c, w_ada, b_ada.reshape(1, n))


def _modnorm_kernel(x_ref, g_ref, sc_ref, sh_ref, o_ref):
    x = x_ref[0]
    xn = x * lax.rsqrt(jnp.mean(x * x, axis=-1, keepdims=True) + NORM_EPS)
    h = (xn * g_ref[...]) * (1.0 + sc_ref[0, 0]) + sh_ref[0, 0]
    o_ref[...] = h.astype(o_ref.dtype)


def _modnorm(x3, g, mod4, scale_idx, shift_idx, tt):
    bm, t, d = x3.shape
    r = mod4.shape[2]
    assert r == 1 or (r == t and tt == t)
    nt = t // tt
    return pl.pallas_call(
        _modnorm_kernel,
        grid=(bm, nt),
        in_specs=[pl.BlockSpec((1, tt, d), lambda b, i: (b, i, 0)),
                  pl.BlockSpec((1, d), lambda b, i: (0, 0)),
                  pl.BlockSpec((1, 1, r, d), lambda b, i: (scale_idx, b, 0, 0)),
                  pl.BlockSpec((1, 1, r, d), lambda b, i: (shift_idx, b, 0, 0))],
        out_specs=pl.BlockSpec((tt, d), lambda b, i: (b * nt + i, 0)),
        out_shape=jax.ShapeDtypeStruct((bm * t, d), BF16),
        compiler_params=_params(("parallel", "parallel")),
        name="modnorm",
    )(x3, g.reshape(1, d), mod4, mod4)


def _mm_kernel(a_ref, w_ref, o_ref):
    o_ref[...] = _dot(a_ref[...], w_ref[...]).astype(o_ref.dtype)


def _matmul(a, w, tm, tn, out_dtype=F32):
    m, k = a.shape
    n = w.shape[1]
    tm = min(tm, m)
    tn = min(tn, n)
    return pl.pallas_call(
        _mm_kernel,
        grid=(m // tm, n // tn),
        in_specs=[pl.BlockSpec((tm, k), lambda i, j: (i, 0)),
                  pl.BlockSpec((k, tn), lambda i, j: (0, j))],
        out_specs=pl.BlockSpec((tm, tn), lambda i, j: (i, j)),
        out_shape=jax.ShapeDtypeStruct((m, n), out_dtype),
        compiler_params=_params(("parallel", "parallel")),
        name="matmul",
    )(a, w)


def _dt_kernel(a_ref, w_ref, b_ref, o_ref):
    o_ref[...] = _softplus(_dot(a_ref[...], w_ref[...]) + b_ref[...])


def _dt_proj(a, w, bias, tm):
    m, k = a.shape
    n = w.shape[1]
    tm = min(tm, m)
    return pl.pallas_call(
        _dt_kernel, grid=(m // tm,),
        in_specs=[pl.BlockSpec((tm, k), lambda i: (i, 0)),
                  pl.BlockSpec((k, n), lambda i: (0, 0)),
                  pl.BlockSpec((1, n), lambda i: (0, 0))],
        out_specs=pl.BlockSpec((tm, n), lambda i: (i, 0)),
        out_shape=jax.ShapeDtypeStruct((m, n), F32),
        compiler_params=_params(("parallel",)),
        name="dt_proj",
    )(a, w, bias)


def _moba_prompt_kernel(slopes_ref, q_ref, k_ref, v_ref, o_ref,
                        kb_ref, vb_ref, km_ref, bias_ref, bias_own_ref, s_ref, *, n_blk, scale):
    h = pl.program_id(1)
    ob = pl.program_id(2)
    blk = MOBA_BLOCK
    slope = slopes_ref[h]

    @pl.when(ob == 0)
    def _():
        k = k_ref[0]
        kb_ref[...] = k.astype(BF16)
        vb_ref[...] = v_ref[0].astype(BF16)
        for j in range(n_blk):
            km_ref[j:j + 1, :] = jnp.sum(k[j * blk:(j + 1) * blk], axis=0, keepdims=True) * (1.0 / blk)
        row = lax.broadcasted_iota(jnp.int32, (blk, blk), 0)
        col = lax.broadcasted_iota(jnp.int32, (blk, blk), 1)
        b0 = slope * (row - col).astype(F32)
        bias_ref[...] = b0
        bias_own_ref[...] = jnp.where(col <= row, b0, float("inf"))

    q = q_ref[0]
    qb = q.astype(BF16)

    gt = _dot_hi_nt(km_ref[...], q)
    jrow = lax.broadcasted_iota(jnp.int32, gt.shape, 0)
    gt = jnp.where(jrow < ob, gt, NEG_INF)
    pen_rows = []
    for j in range(n_blk):
        gj = gt[j:j + 1, :]
        beats = (gt > gj) | ((gt == gj) & (jrow < j))
        rank = jnp.sum(jnp.where(beats, 1.0, 0.0), axis=0, keepdims=True)
        pen_rows.append(jnp.where(rank < MOBA_TOPK, 0.0, NEG_INF))
    pen_rows.append(jnp.zeros((LANES - n_blk, blk), F32))
    pen = jnp.concatenate(pen_rows, axis=0).T

    def attend(n):
        mx = None
        for jj in range(n + 1):
            qk = _dot_nt(qb, kb_ref[jj * blk:(jj + 1) * blk, :]) * scale
            if jj == n:
                s = qk - bias_own_ref[...]
            else:
                s = (qk - bias_ref[...]) + (pen[:, jj:jj + 1] - slope * float((n - jj) * blk))
            s_ref[jj] = s
            part = jnp.maximum(s[:, :LANES], s[:, LANES:])
            mx = part if mx is None else jnp.maximum(mx, part)
        m = jnp.max(mx, axis=1, keepdims=True)
        lsum = None
        acc = None
        for jj in range(n + 1):
            p = jnp.exp(s_ref[jj] - m)
            part = p[:, :LANES] + p[:, LANES:]
            lsum = part if lsum is None else lsum + part
            pv = _dot(p.astype(BF16), vb_ref[jj * blk:(jj + 1) * blk, :])
            acc = pv if acc is None else acc + pv
        l = jnp.sum(lsum, axis=1, keepdims=True)
        o_ref[0] = (acc / l).astype(o_ref.dtype)

    for n in range(n_blk):
        pl.when(ob == n)(functools.partial(attend, n))


def _moba_prompt(q3, k3, v3, slopes):
    b, t, w = q3.shape
    hd = w // N_HEADS
    assert t % MOBA_BLOCK == 0 and hd == LANES
    n_blk = t // MOBA_BLOCK
    assert n_blk >= MOBA_TOPK
    kern = functools.partial(_moba_prompt_kernel, n_blk=n_blk, scale=hd ** -0.5)
    grid_spec = pltpu.PrefetchScalarGridSpec(
        num_scalar_prefetch=1,
        grid=(b, N_HEADS, n_blk),
        in_specs=[pl.BlockSpec((1, MOBA_BLOCK, hd), lambda bi, h, i, s: (bi, i, h)),
                  pl.BlockSpec((1, t, hd), lambda bi, h, i, s: (bi, 0, h)),
                  pl.BlockSpec((1, t, hd), lambda bi, h, i, s: (bi, 0, h))],
        out_specs=pl.BlockSpec((1, MOBA_BLOCK, hd), lambda bi, h, i, s: (bi, i, h)),
        scratch_shapes=[pltpu.VMEM((t, hd), BF16), pltpu.VMEM((t, hd), BF16),
                        pltpu.VMEM((n_blk, hd), F32),
                        pltpu.VMEM((MOBA_BLOCK, MOBA_BLOCK), F32), pltpu.VMEM((MOBA_BLOCK, MOBA_BLOCK), F32),
                        pltpu.VMEM((n_blk, MOBA_BLOCK, MOBA_BLOCK), F32)])
    return pl.pallas_call(
        kern, grid_spec=grid_spec,
        out_shape=jax.ShapeDtypeStruct((b, t, w), BF16),
        compiler_params=_params(("parallel", "parallel", "arbitrary")),
        name="moba_prompt",
    )(slopes, q3, k3, v3)


BLOCKS_PER_MEAN_STEP = 4


def _kmean_kernel(pt_ref, *refs):
    page_refs, o_ref = refs[:-1], refs[-1]
    ppb = MOBA_BLOCK // PAGE_SIZE
    for i in range(BLOCKS_PER_MEAN_STEP):
        s = jnp.sum(page_refs[ppb * i][0, 0], axis=0)
        for p in range(1, ppb):
            s = s + jnp.sum(page_refs[ppb * i + p][0, 0], axis=0)
        o_ref[0, i] = s * (1.0 / MOBA_BLOCK)


def _cache_block_means(cache_k5, layer, page_table):
    bd, n_pages = page_table.shape
    _, _, page, nh, hd = cache_k5.shape
    ppb = MOBA_BLOCK // page
    n_full = n_pages // ppb
    assert n_full % BLOCKS_PER_MEAN_STEP == 0
    pps = ppb * BLOCKS_PER_MEAN_STEP

    def page_spec(i):
        return pl.BlockSpec((1, 1, page, nh, hd),
                            lambda b, j, pt, i=i: (layer, pt[b * n_pages + pps * j + i], 0, 0, 0))

    grid_spec = pltpu.PrefetchScalarGridSpec(
        num_scalar_prefetch=1,
        grid=(bd, n_full // BLOCKS_PER_MEAN_STEP),
        in_specs=[page_spec(i) for i in range(pps)],
        out_specs=pl.BlockSpec((1, BLOCKS_PER_MEAN_STEP, nh, hd), lambda b, j, pt: (b, j, 0, 0)))
    return pl.pallas_call(
        _kmean_kernel, grid_spec=grid_spec,
        out_shape=jax.ShapeDtypeStruct((bd, n_full, nh, hd), F32),
        compiler_params=_params(("parallel", "parallel")),
        name="cache_block_means",
    )(page_table.reshape(-1), *([cache_k5] * pps))


def _sample_topk_kernel(q_ref, km_ref, o_ref, *, n_full):
    q = q_ref[0]
    rows = q.shape[0]
    lane = lax.broadcasted_iota(jnp.int32, (rows, n_full), 1).astype(F32)
    lane_o = lax.broadcasted_iota(jnp.int32, (rows, LANES), 1)
    for h in range(N_HEADS):
        g = _dot_hi_nt(q[:, h * LANES:(h + 1) * LANES], km_ref[0, :, h, :])
        res = jnp.zeros((rows, LANES), F32)
        for s in range(MOBA_TOPK):
            mx = jnp.max(g, axis=1, keepdims=True)
            idx = jnp.min(jnp.where(g == mx, lane, float(n_full)), axis=1, keepdims=True)
            res = jnp.where(lane_o == s, idx, res)
            g = jnp.where(lane == idx, NEG_INF, g)
        o_ref[0, h * rows:(h + 1) * rows, :] = res.astype(jnp.int32)


def _sample_topk(q_pad, kmean):
    bd, rows, w = q_pad.shape
    _, n_full, nh, hd = kmean.shape
    assert n_full >= MOBA_TOPK and nh == N_HEADS and hd == LANES
    kern = functools.partial(_sample_topk_kernel, n_full=n_full)
    return pl.pallas_call(
        kern, grid=(bd,),
        in_specs=[pl.BlockSpec((1, rows, w), lambda b: (b, 0, 0)),
                  pl.BlockSpec((1, n_full, nh, hd), lambda b: (b, 0, 0, 0))],
        out_specs=pl.BlockSpec((1, N_HEADS * rows, LANES), lambda b: (b, 0, 0)),
        out_shape=jax.ShapeDtypeStruct((bd, N_HEADS * rows, LANES), jnp.int32),
        compiler_params=_params(("parallel",)),
        name="sample_topk",
    )(q_pad, kmean)


def _sample_attn_kernel(pt_ref, sel_ref, slopes_ref, q_ref, kn_ref, vn_ref, *rest,
                        tq, past, n_full, scale):
    ppb = MOBA_BLOCK // PAGE_SIZE
    k_refs = rest[:ppb]
    v_refs = rest[ppb:2 * ppb]
    o_ref, qs_ref, bias_ref, m_ref, l_ref, acc_ref = rest[2 * ppb:]
    b = pl.program_id(0)
    j = pl.program_id(1)
    rows = q_ref.shape[1]
    n_rows = N_HEADS * rows
    n_keys = MOBA_BLOCK * N_HEADS
    head_of_row = lax.broadcasted_iota(jnp.int32, (n_rows, 1), 0) // rows
    q_of_row = lax.broadcasted_iota(jnp.int32, (n_rows, 1), 0) % rows
    slope_col = jnp.zeros((n_rows, 1), F32)
    for h in range(N_HEADS):
        slope_col = jnp.where(head_of_row == h, slopes_ref[h], slope_col)

    @pl.when(j == 0)
    def _():
        q = q_ref[0]
        r4 = lax.broadcasted_iota(jnp.int32, (rows, tq), 0)
        c4 = lax.broadcasted_iota(jnp.int32, (rows, tq), 1)
        for h in range(N_HEADS):
            sl = slice(h * LANES, (h + 1) * LANES)
            hs = slice(h * rows, (h + 1) * rows)
            qs_ref[hs, :] = q[:, sl].astype(BF16)
            kn = kn_ref[0, :, sl]
            vn = vn_ref[0, :, sl]
            s = _dot_nt(q[:, sl].astype(BF16), kn.astype(BF16)) * scale - slopes_ref[h] * (r4 - c4).astype(F32)
            s = jnp.where(c4 <= r4, s, NEG_INF)
            m = jnp.max(s, axis=1, keepdims=True)
            p = jnp.exp(s - m)
            acc = p[:, 0:1] * vn[0:1, :]
            for c in range(1, tq):
                acc = acc + p[:, c:c + 1] * vn[c:c + 1, :]
            m_ref[hs, :] = m
            l_ref[hs, :] = jnp.sum(p, axis=1, keepdims=True)
            acc_ref[hs, :] = acc
        key = lax.broadcasted_iota(jnp.int32, (n_rows, n_keys), 1)
        rel = (q_of_row - key // N_HEADS).astype(F32)
        bias_ref[...] = jnp.where(key % N_HEADS == head_of_row, slope_col * rel, float("inf"))

    bits_col = jnp.zeros((n_rows, 1), jnp.int32)
    for h in range(N_HEADS):
        bits_col = jnp.where(head_of_row == h, sel_ref[(b * N_HEADS + h) * n_full + j], bits_col)
    picked = ((bits_col >> q_of_row) & 1) == 1
    offset = (past - j * MOBA_BLOCK).astype(F32)
    t_col = jnp.where(picked, -(slope_col * offset), NEG_INF)

    kp = jnp.concatenate([r[0, 0].reshape(PAGE_SIZE * N_HEADS, LANES) for r in k_refs], axis=0).astype(BF16)
    vp = jnp.concatenate([r[0, 0].reshape(PAGE_SIZE * N_HEADS, LANES) for r in v_refs], axis=0).astype(BF16)
    s = _dot_nt(qs_ref[...], kp) * scale - bias_ref[...]
    m_old = m_ref[...]
    m_new = jnp.maximum(m_old, jnp.max(s, axis=1, keepdims=True) + t_col)
    alpha = jnp.exp(m_old - m_new)
    p = jnp.exp(s - (m_new - t_col))
    l_ref[...] = alpha * l_ref[...] + jnp.sum(p, axis=1, keepdims=True)
    acc_ref[...] = alpha * acc_ref[...] + _dot(p.astype(BF16), vp)
    m_ref[...] = m_new

    @pl.when(j == n_full - 1)
    def _():
        for h in range(N_HEADS):
            hs = slice(h * rows, (h + 1) * rows)
            o_ref[0, :, h * LANES:(h + 1) * LANES] = acc_ref[hs, :] / l_ref[hs, :]


def _sample_attention(q_pad, k_new, v_new, cache_k5, cache_v5, layer, page_table, sel_bits, slopes):
    bd, rows, w = q_pad.shape
    tq = k_new.shape[1]
    n_pages = page_table.shape[1]
    _, _, page, nh, hd = cache_k5.shape
    ppb = MOBA_BLOCK // page
    n_full = n_pages // ppb
    assert nh == N_HEADS and hd == LANES and n_pages % ppb == 0

    def page_spec(i):
        return pl.BlockSpec((1, 1, page, nh, hd),
                            lambda b, j, pt, sel, sl, i=i: (layer, pt[b * n_pages + ppb * j + i], 0, 0, 0))

    grid_spec = pltpu.PrefetchScalarGridSpec(
        num_scalar_prefetch=3,
        grid=(bd, n_full),
        in_specs=[pl.BlockSpec((1, rows, w), lambda b, j, *_: (b, 0, 0)),
                  pl.BlockSpec((1, tq, w), lambda b, j, *_: (b, 0, 0)),
                  pl.BlockSpec((1, tq, w), lambda b, j, *_: (b, 0, 0))]
        + [page_spec(i) for i in range(ppb)] + [page_spec(i) for i in range(ppb)],
        out_specs=pl.BlockSpec((1, rows, w), lambda b, j, *_: (b, 0, 0)),
        scratch_shapes=[pltpu.VMEM((N_HEADS * rows, hd), BF16),
                        pltpu.VMEM((N_HEADS * rows, MOBA_BLOCK * N_HEADS), F32),
                        pltpu.VMEM((N_HEADS * rows, 1), F32), pltpu.VMEM((N_HEADS * rows, 1), F32),
                        pltpu.VMEM((N_HEADS * rows, hd), F32)])
    kern = functools.partial(_sample_attn_kernel, tq=tq, past=n_pages * page, n_full=n_full, scale=hd ** -0.5)
    return pl.pallas_call(
        kern, grid_spec=grid_spec,
        out_shape=jax.ShapeDtypeStruct((bd, rows, w), F32),
        compiler_params=_params(("parallel", "arbitrary")),
        name="sample_attention",
    )(page_table.reshape(-1), sel_bits, slopes, q_pad, k_new, v_new,
      *([cache_k5] * ppb), *([cache_v5] * ppb))


def _moba_sample(qs, ks, vs, cache_k5, cache_v5, layer, page_table, slopes):
    bd, tq, w = qs.shape
    n_pages = page_table.shape[1]
    assert (n_pages * PAGE_SIZE) % MOBA_BLOCK == 0 and n_pages > 0 and tq <= 8
    n_full = n_pages * PAGE_SIZE // MOBA_BLOCK
    kmean = _cache_block_means(cache_k5, layer, page_table)
    q_pad = jnp.pad(qs, ((0, 0), (0, 8 - tq), (0, 0)))
    top = _sample_topk(q_pad, kmean).reshape(bd, N_HEADS, 8, LANES)[:, :, :tq, :MOBA_TOPK]
    hit = jnp.any(top[..., None] == jnp.arange(n_full, dtype=jnp.int32), axis=3)
    sel_bits = jnp.sum(jnp.where(hit, 1 << jnp.arange(tq, dtype=jnp.int32)[:, None], 0), axis=2)
    out = _sample_attention(q_pad, ks, vs, cache_k5, cache_v5, layer, page_table,
                            sel_bits.reshape(-1).astype(jnp.int32), slopes)
    return out[:, :tq, :]


def _ssd_kernel(x_ref, b_ref, c_ref, wx_ref, wb_ref, wc_ref, bx_ref, bb_ref, bc_ref,
                ix_ref, ib_ref, ic_ref, dt_ref, dtt_ref,
                alog_row_ref, alog_col_ref, dskip_ref, z_ref, gssm_ref, s0_ref,
                y_ref, sfin_ref, carry_ref, cbuf_ref, ubuf_ref, *, n_groups, d_inner):
    L = SSD_CHUNK
    gw = x_ref.shape[2]
    n_state = b_ref.shape[2]
    hpg = gw // SSM_HEADDIM
    cw = gw + 2 * n_state
    c = pl.program_id(1)
    g = pl.program_id(2)
    rows_g = pl.ds(pl.multiple_of(g * gw, gw), gw)

    @pl.when(c == 0)
    def _():
        carry_ref[g] = jnp.zeros((8, cw), F32)
        carry_ref[g, 5:8, 0:gw] = ix_ref[0]
        carry_ref[g, 5:8, gw:gw + n_state] = ib_ref[0]
        carry_ref[g, 5:8, gw + n_state:cw] = ic_ref[0]
        sfin_ref[0, rows_g, :] = s0_ref[0, rows_g, :]

    cbuf_ref[0:8, :] = carry_ref[g]
    cbuf_ref[8:8 + L, 0:gw] = x_ref[0]
    cbuf_ref[8:8 + L, gw:gw + n_state] = b_ref[0]
    cbuf_ref[8:8 + L, gw + n_state:cw] = c_ref[0]
    carry_ref[g] = cbuf_ref[L:L + 8, :]
    w = jnp.concatenate([wx_ref[...], wb_ref[...], wc_ref[...]], axis=1)
    bias = jnp.concatenate([bx_ref[...], bb_ref[...], bc_ref[...]], axis=1)
    window = cbuf_ref[...]
    conv = bias + w[3:4] * window[8:8 + L]
    for k in range(1, 4):
        conv = conv + w[3 - k:4 - k] * pltpu.roll(window, k, 0)[8:8 + L]
    act = _silu(conv)
    xs = act[:, 0:gw]
    bm = act[:, gw:gw + n_state].astype(BF16)
    cm = act[:, gw + n_state:cw].astype(BF16)

    dt = dt_ref[0, 0]
    dtt = dtt_ref[0, 0]
    adt = dt * (-jnp.exp(alog_row_ref[0]))
    adtt = dtt * (-jnp.exp(alog_col_ref[0]))

    row = lax.broadcasted_iota(jnp.int32, (L, L), 0)
    col = lax.broadcasted_iota(jnp.int32, (L, L), 1)
    causal = col <= row
    tri_l = jnp.where(causal, 1.0, 0.0).astype(BF16)
    tri_u = jnp.where(row <= col, 1.0, 0.0).astype(BF16)
    acs_col = _dot_exact_lhs(tri_l, adt)
    acs_row = _dot_exact_rhs(adtt, tri_u)

    er = lax.broadcasted_iota(jnp.int32, (LANES, gw), 0)
    ec = lax.broadcasted_iota(jnp.int32, (LANES, gw), 1)
    expand = jnp.where(er == ec // SSM_HEADDIM, 1.0, 0.0).astype(BF16)
    eacs = jnp.exp(acs_col)
    dte = jnp.exp(acs_col[L - 1:L, :] - acs_col)
    st_hi, st_lo = _split2(jnp.concatenate([dt, eacs, dte], axis=0))
    stacked = _dot(st_hi, expand) + _dot(st_lo, expand)
    dt_x = stacked[0:L]
    eacs_x = stacked[L:2 * L]
    dte_x = stacked[2 * L:3 * L]

    xdt = xs * dt_x
    xdt_b = xdt.astype(BF16)
    cb = _dot_nt(cm, bm)
    lane = lax.broadcasted_iota(jnp.int32, (L, LANES), 1)
    pairs = []
    for i in range(hpg // 2):
        xp = xdt_b[:, i * LANES:(i + 1) * LANES]
        res = []
        for j in (2 * i, 2 * i + 1):
            diff = acs_col[:, j:j + 1] - acs_row[j:j + 1, :]
            decay = jnp.exp(jnp.where(causal, diff, NEG_INF))
            res.append(_dot((cb * decay).astype(BF16), xp))
        pairs.append(jnp.where(lane < SSM_HEADDIM, res[0], res[1]))
    y_diag = jnp.concatenate(pairs, axis=1)

    s_prev = sfin_ref[0, rows_g, :]
    y_off = _dot_nt(cm, s_prev.astype(BF16)) * eacs_x
    y = y_diag + y_off + dskip_ref[...] * xs
    ubuf_ref[g] = y * _silu(z_ref[0])

    xw_t = (xdt * dte_x).T.astype(BF16)
    s_chunk = _dot(xw_t, bm)
    last = jnp.broadcast_to(acs_row[:, L - 1:L], (hpg, n_state))
    last = jnp.concatenate([last, jnp.zeros((LANES - hpg, n_state), F32)], axis=0)
    tr = lax.broadcasted_iota(jnp.int32, (gw, LANES), 0)
    tc = lax.broadcasted_iota(jnp.int32, (gw, LANES), 1)
    expand_t = jnp.where(tc == tr // SSM_HEADDIM, 1.0, 0.0).astype(BF16)
    chunk_decay = jnp.exp(_dot_exact_lhs(expand_t, last))
    sfin_ref[0, rows_g, :] = chunk_decay * s_prev + s_chunk

    @pl.when(g == n_groups - 1)
    def _():
        ss = jnp.zeros((L, 1), F32)
        for gi in range(n_groups):
            u = ubuf_ref[gi]
            ss = ss + jnp.sum(u * u, axis=1, keepdims=True)
        r = lax.rsqrt(ss * (1.0 / d_inner) + NORM_EPS)
        for gi in range(n_groups):
            yn = (ubuf_ref[gi] * r) * gssm_ref[:, gi * gw:(gi + 1) * gw]
            y_ref[0, :, gi * gw:(gi + 1) * gw] = yn.astype(y_ref.dtype)


def _ssd(xbc3, z3, dt_g, dtt_g, conv_init, ssm_init, conv_w, conv_b, a_log, d_skip, g_ssm):
    bm, t, conv_dim = xbc3.shape
    d_inner = z3.shape[2]
    G = SSM_GROUPS
    n_state = SSM_STATE
    gw = d_inner // G
    hpg = gw // SSM_HEADDIM
    assert hpg % 2 == 0 and hpg <= 8 and n_state == LANES and 2 * SSM_HEADDIM == LANES
    L = SSD_CHUNK
    assert t % L == 0
    nc = t // L
    cw = gw + 2 * n_state
    xb = d_inner // n_state
    cbk = xb + G
    pad = lambda a: jnp.pad(a, ((0, 0), (0, 0), (0, LANES - a.shape[2])))
    alog_row = pad(a_log.reshape(G, 1, hpg))
    alog_col = a_log.reshape(G, hpg, 1)
    dskip_x = jnp.repeat(d_skip, SSM_HEADDIM).reshape(1, d_inner)
    conv_b2 = conv_b.reshape(1, conv_dim)
    kern = functools.partial(_ssd_kernel, n_groups=G, d_inner=d_inner)
    in_specs = [
        pl.BlockSpec((1, L, gw), lambda b, c, g: (b, c, g)),
        pl.BlockSpec((1, L, n_state), lambda b, c, g: (b, c, xb + g)),
        pl.BlockSpec((1, L, n_state), lambda b, c, g: (b, c, cbk + g)),
        pl.BlockSpec((4, gw), lambda b, c, g: (0, g)),
        pl.BlockSpec((4, n_state), lambda b, c, g: (0, xb + g)),
        pl.BlockSpec((4, n_state), lambda b, c, g: (0, cbk + g)),
        pl.BlockSpec((1, gw), lambda b, c, g: (0, g)),
        pl.BlockSpec((1, n_state), lambda b, c, g: (0, xb + g)),
        pl.BlockSpec((1, n_state), lambda b, c, g: (0, cbk + g)),
        pl.BlockSpec((1, 3, gw), lambda b, c, g: (b, 0, g)),
        pl.BlockSpec((1, 3, n_state), lambda b, c, g: (b, 0, xb + g)),
        pl.BlockSpec((1, 3, n_state), lambda b, c, g: (b, 0, cbk + g)),
        pl.BlockSpec((1, 1, L, LANES), lambda b, c, g: (b, g, c, 0)),
        pl.BlockSpec((1, 1, hpg, L), lambda b, c, g: (b, g, 0, c)),
        pl.BlockSpec((1, 1, LANES), lambda b, c, g: (g, 0, 0)),
        pl.BlockSpec((1, hpg, 1), lambda b, c, g: (g, 0, 0)),
        pl.BlockSpec((1, gw), lambda b, c, g: (0, g)),
        pl.BlockSpec((1, L, gw), lambda b, c, g: (b, c, g)),
        pl.BlockSpec((1, d_inner), lambda b, c, g: (0, 0)),
        pl.BlockSpec((1, d_inner, n_state), lambda b, c, g: (b, 0, 0)),
    ]
    out_specs = [pl.BlockSpec((1, L, d_inner), lambda b, c, g: (b, c, 0)),
                 pl.BlockSpec((1, d_inner, n_state), lambda b, c, g: (b, 0, 0))]
    return pl.pallas_call(
        kern, grid=(bm, nc, G), in_specs=in_specs, out_specs=out_specs,
        out_shape=[jax.ShapeDtypeStruct((bm, t, d_inner), BF16),
                   jax.ShapeDtypeStruct((bm, d_inner, n_state), F32)],
        scratch_shapes=[pltpu.VMEM((G, 8, cw), F32), pltpu.VMEM((8 + L, cw), F32),
                        pltpu.VMEM((G, L, gw), F32)],
        compiler_params=_params(("parallel", "arbitrary", "arbitrary")),
        name="ssd",
    )(xbc3, xbc3, xbc3, conv_w, conv_w, conv_w, conv_b2, conv_b2, conv_b2,
      conv_init, conv_init, conv_init, dt_g, dtt_g, alog_row, alog_col,
      dskip_x, z3, g_ssm.reshape(1, d_inner), ssm_init)


def _merge_kernel(a_ref, y_ref, wa_ref, ws_ref, ga_ref, gs_ref, ba_ref, bs_ref, o_ref):
    pa = _dot(a_ref[...], wa_ref[...])
    ps = _dot(y_ref[...], ws_ref[...])
    o = _sigmoid(ga_ref[...] + ba_ref[...]) * pa + _sigmoid(gs_ref[...] + bs_ref[...]) * ps
    o_ref[...] = o.astype(o_ref.dtype)


def _merge(attn, y, w_a, w_s, ga, gs, b_gate, tm, tn):
    m, ka = attn.shape
    ks = y.shape[1]
    n = w_a.shape[1]
    tm = min(tm, m)
    ba = b_gate[:n].reshape(1, n)
    bs = b_gate[n:].reshape(1, n)
    return pl.pallas_call(
        _merge_kernel, grid=(m // tm, n // tn),
        in_specs=[pl.BlockSpec((tm, ka), lambda i, j: (i, 0)),
                  pl.BlockSpec((tm, ks), lambda i, j: (i, 0)),
                  pl.BlockSpec((ka, tn), lambda i, j: (0, j)),
                  pl.BlockSpec((ks, tn), lambda i, j: (0, j)),
                  pl.BlockSpec((tm, tn), lambda i, j: (i, j)),
                  pl.BlockSpec((tm, tn), lambda i, j: (i, j)),
                  pl.BlockSpec((1, tn), lambda i, j: (0, j)),
                  pl.BlockSpec((1, tn), lambda i, j: (0, j))],
        out_specs=pl.BlockSpec((tm, tn), lambda i, j: (i, j)),
        out_shape=jax.ShapeDtypeStruct((m, n), BF16),
        compiler_params=_params(("parallel", "parallel")),
        name="merge",
    )(attn, y, w_a, w_s, ga, gs, ba, bs)


def _outproj_route_kernel(mg_ref, wout_ref, x_ref, g1_ref, sc2_ref, sh2_ref, gffn_ref,
                          wgh_ref, wgl_ref, bg_ref, wrh_ref, wrl_ref, br_ref,
                          x1_ref, h2_ref, ids_ref, ew_ref, cnt_ref, carry_ref):
    i = pl.program_id(0)

    @pl.when(i == 0)
    def _():
        carry_ref[...] = jnp.zeros_like(carry_ref)

    x1 = x_ref[...] + g1_ref[0, 0] * _dot(mg_ref[...], wout_ref[...])
    x1_ref[...] = x1
    xn = x1 * lax.rsqrt(jnp.mean(x1 * x1, axis=-1, keepdims=True) + NORM_EPS)
    h2 = (xn * gffn_ref[...]) * (1.0 + sc2_ref[0, 0]) + sh2_ref[0, 0]
    h2_ref[...] = h2

    hh, hl = _split2(h2)
    lg = _dot(hh, wgh_ref[...]) + _dot(hh, wgl_ref[...]) + _dot(hl, wgh_ref[...]) + bg_ref[...]
    le = _dot(hh, wrh_ref[...]) + _dot(hh, wrl_ref[...]) + _dot(hl, wrh_ref[...]) + br_ref[...]
    tm = lg.shape[0]
    lane_i = lax.broadcasted_iota(jnp.int32, (tm, LANES), 1)
    lane = lane_i.astype(F32)
    in_g = lane_i < N_EXPERT_GROUPS
    lgm = jnp.where(in_g, lg, NEG_INF)
    mg = jnp.max(lgm, axis=1, keepdims=True)
    gtop = jnp.min(jnp.where(lgm == mg, lane, float(LANES)), axis=1, keepdims=True)
    gw = 1.0 / jnp.sum(jnp.where(in_g, jnp.exp(lg - mg), 0.0), axis=1, keepdims=True)
    grp = (lane_i // EXPERTS_PER_GROUP).astype(F32)
    lem = jnp.where((lane_i < N_EXPERTS) & (grp == gtop), le, NEG_INF)
    e1 = jnp.max(lem, axis=1, keepdims=True)
    i1 = jnp.min(jnp.where(lem == e1, lane, float(LANES)), axis=1, keepdims=True)
    lem2 = jnp.where(lane == i1, NEG_INF, lem)
    e2 = jnp.max(lem2, axis=1, keepdims=True)
    i2 = jnp.min(jnp.where(lem2 == e2, lane, float(LANES)), axis=1, keepdims=True)
    t2 = jnp.exp(e2 - e1)
    den = 1.0 + t2
    w1 = (1.0 / den) * gw
    w2 = (t2 / den) * gw

    hit1 = lane == i1
    hit2 = lane == i2
    onehot = jnp.where(hit1 | hit2, 1.0, 0.0)
    rr = lax.broadcasted_iota(jnp.int32, (tm, tm), 0)
    cc = lax.broadcasted_iota(jnp.int32, (tm, tm), 1)
    strict = jnp.where(cc < rr, 1.0, 0.0).astype(BF16)
    cum = _dot(strict, onehot.astype(BF16)) + carry_ref[...]
    r1 = jnp.sum(jnp.where(hit1, cum, 0.0), axis=1, keepdims=True)
    r2 = jnp.sum(jnp.where(hit2, cum, 0.0), axis=1, keepdims=True)
    carry_ref[...] = carry_ref[...] + jnp.sum(onehot, axis=0, keepdims=True)
    cnt_ref[...] = carry_ref[...]
    packed = jnp.where(lane_i == 0, i1, jnp.where(lane_i == 1, i2,
                       jnp.where(lane_i == 2, r1, jnp.where(lane_i == 3, r2, 0.0))))
    ids_ref[...] = packed.astype(jnp.int32)
    ew_ref[...] = jnp.where(lane_i == 0, w1, jnp.where(lane_i == 1, w2, 0.0))


def _outproj_route(merged, w_out, x2, mod4, t_seq, g_ffn, w_group, b_group, w_router, b_router, tm):
    m, d = x2.shape
    r = mod4.shape[2]
    assert t_seq % tm == 0 and (r == 1 or tm == t_seq)
    per_seq = t_seq // tm

    def pad_cols(w):
        return jnp.pad(w, ((0, 0), (0, LANES - w.shape[1])))

    wg = pad_cols(w_group)
    wr = pad_cols(w_router)
    wgh, wgl = _split2(wg)
    wrh, wrl = _split2(wr)
    bg = pad_cols(b_group.reshape(1, -1))
    br = pad_cols(b_router.reshape(1, -1))
    row_spec = pl.BlockSpec((tm, d), lambda i: (i, 0))
    full = lambda shape: pl.BlockSpec(shape, lambda i: tuple(0 for _ in shape))
    mod_spec = lambda idx: pl.BlockSpec((1, 1, r, d), lambda i: (idx, i // per_seq, 0, 0))
    lane_spec = pl.BlockSpec((tm, LANES), lambda i: (i, 0))
    return pl.pallas_call(
        _outproj_route_kernel, grid=(m // tm,),
        in_specs=[row_spec, full((d, d)), row_spec, mod_spec(2), mod_spec(4), mod_spec(3), full((1, d)),
                  full((d, LANES)), full((d, LANES)), full((1, LANES)),
                  full((d, LANES)), full((d, LANES)), full((1, LANES))],
        out_specs=[row_spec, row_spec, lane_spec, lane_spec, full((1, LANES))],
        out_shape=[jax.ShapeDtypeStruct((m, d), F32), jax.ShapeDtypeStruct((m, d), F32),
                   jax.ShapeDtypeStruct((m, LANES), jnp.int32), jax.ShapeDtypeStruct((m, LANES), F32),
                   jax.ShapeDtypeStruct((1, LANES), F32)],
        scratch_shapes=[pltpu.VMEM((1, LANES), F32)],
        compiler_params=_params(("arbitrary",)),
        name="outproj_route",
    )(merged, w_out, x2, mod4, mod4, mod4, g_ffn.reshape(1, d), wgh, wgl, bg, wrh, wrl, br)


DMA_UNROLL = 8


def _row_copy(src_hbm, row, dst, slot, sem):
    return pltpu.make_async_copy(src_hbm.at[pl.ds(row, 1), :], dst.at[pl.ds(slot, 1), :], sem)


def _expert_kernel(bexp_ref, nb_ref, tok_ref, h2_hbm, wg_ref, wu_ref, wd_ref, o_ref, xbuf, sems):
    blk = pl.program_id(0)
    n_used = nb_ref[0]

    def gather(block, buf):
        base = block * MOE_BLOCK

        def issue(r, carry):
            _row_copy(h2_hbm, tok_ref[base + r], xbuf.at[buf], r, sems.at[buf]).start()
            return carry

        lax.fori_loop(0, MOE_BLOCK, issue, 0, unroll=DMA_UNROLL)

    @pl.when(blk == 0)
    def _():
        gather(0, 0)

    @pl.when(blk < n_used)
    def _():
        cur = blk % 2

        @pl.when(blk + 1 < n_used)
        def _():
            gather(blk + 1, 1 - cur)

        def drain(r, carry):
            _row_copy(h2_hbm, 0, xbuf.at[cur], r, sems.at[cur]).wait()
            return carry

        lax.fori_loop(0, MOE_BLOCK, drain, 0, unroll=DMA_UNROLL)
        x = xbuf[cur].astype(BF16)
        gate = _dot(x, wg_ref[0])
        up = _dot(x, wu_ref[0])
        act = (_silu(gate) * up).astype(BF16)
        o_ref[...] = _dot(act, wd_ref[0])

    @pl.when(blk >= n_used)
    def _():
        o_ref[...] = jnp.zeros_like(o_ref)


def _experts(h2, slot_tok, block_expert, n_used, w_gate_b, w_up_b, w_down_b):
    t, d = h2.shape
    n_slots = slot_tok.shape[0]
    n_blocks = n_slots // MOE_BLOCK
    hid = w_up_b.shape[2]
    grid_spec = pltpu.PrefetchScalarGridSpec(
        num_scalar_prefetch=3,
        grid=(n_blocks,),
        in_specs=[pl.BlockSpec(memory_space=pl.ANY),
                  pl.BlockSpec((1, d, hid), lambda i, be, nb, tok: (be[i], 0, 0)),
                  pl.BlockSpec((1, d, hid), lambda i, be, nb, tok: (be[i], 0, 0)),
                  pl.BlockSpec((1, hid, d), lambda i, be, nb, tok: (be[i], 0, 0))],
        out_specs=pl.BlockSpec((MOE_BLOCK, d), lambda i, be, nb, tok: (i, 0)),
        scratch_shapes=[pltpu.VMEM((2, MOE_BLOCK, d), F32), pltpu.SemaphoreType.DMA((2,))])
    return pl.pallas_call(
        _expert_kernel, grid_spec=grid_spec,
        out_shape=jax.ShapeDtypeStruct((n_slots, d), F32),
        compiler_params=_params(("arbitrary",)),
        name="experts",
    )(block_expert, n_used, slot_tok, h2, w_gate_b, w_up_b, w_down_b)


def _combine_kernel(dest_ref, yb_hbm, ew_ref, x1_ref, g2_ref, gfin_ref, o_ref, buf, sem):
    i = pl.program_id(0)
    tm = x1_ref.shape[0]
    n_rows = buf.shape[0]
    base = i * n_rows

    def issue(r, carry):
        _row_copy(yb_hbm, dest_ref[base + r], buf, r, sem).start()
        return carry

    lax.fori_loop(0, n_rows, issue, 0, unroll=DMA_UNROLL)

    def drain(r, carry):
        _row_copy(yb_hbm, 0, buf, r, sem).wait()
        return carry

    lax.fori_loop(0, n_rows, drain, 0, unroll=DMA_UNROLL)
    w1 = ew_ref[:, 0:1]
    w2 = ew_ref[:, 1:2]
    moe = buf[0:tm, :] * w1 + buf[tm:2 * tm, :] * w2
    x2 = x1_ref[...] + g2_ref[0, 0] * moe
    xn = x2 * lax.rsqrt(jnp.mean(x2 * x2, axis=-1, keepdims=True) + NORM_EPS)
    o_ref[...] = xn * gfin_ref[...]


def _combine(yb, dest_tiles, ew, x1, mod4, t_seq, g_final, tm):
    m, d = x1.shape
    r = mod4.shape[2]
    assert t_seq % tm == 0 and (r == 1 or tm == t_seq)
    per_seq = t_seq // tm
    grid_spec = pltpu.PrefetchScalarGridSpec(
        num_scalar_prefetch=1,
        grid=(m // tm,),
        in_specs=[pl.BlockSpec(memory_space=pl.ANY),
                  pl.BlockSpec((tm, LANES), lambda i, de: (i, 0)),
                  pl.BlockSpec((tm, d), lambda i, de: (i, 0)),
                  pl.BlockSpec((1, 1, r, d), lambda i, de: (5, i // per_seq, 0, 0)),
                  pl.BlockSpec((1, d), lambda i, de: (0, 0))],
        out_specs=pl.BlockSpec((tm, d), lambda i, de: (i, 0)),
        scratch_shapes=[pltpu.VMEM((2 * tm, d), F32), pltpu.SemaphoreType.DMA])
    return pl.pallas_call(
        _combine_kernel, grid_spec=grid_spec,
        out_shape=jax.ShapeDtypeStruct((m, d), F32),
        compiler_params=_params(("arbitrary",)),
        name="combine",
    )(dest_tiles, yb, ew, x1, mod4, g_final.reshape(1, d))


def _moe_and_final_norm(merged, x2, mod4, t_seq, lw, g_final, tm_route, tm_comb):
    m, d = x2.shape
    x1, h2, ids, ew, cnt = _outproj_route(merged, lw["w_out_b"], x2, mod4, t_seq, lw["g_ffn"],
                                          lw["w_group"], lw["b_group"], lw["w_router"], lw["b_router"],
                                          tm_route)
    e_id = ids[:, 0:2]
    rank = ids[:, 2:4]
    counts = cnt[0, :N_EXPERTS].astype(jnp.int32)
    padded = (counts + MOE_BLOCK - 1) // MOE_BLOCK * MOE_BLOCK
    pad_end = jnp.cumsum(padded)
    pad_start = pad_end - padded
    experts = jnp.arange(N_EXPERTS, dtype=jnp.int32)
    dest = jnp.sum(jnp.where(e_id[..., None] == experts, pad_start, 0), axis=-1) + rank
    n_assign = 2 * m
    n_blocks = (n_assign + N_EXPERTS * (MOE_BLOCK - 1) + MOE_BLOCK - 1) // MOE_BLOCK
    n_slots = n_blocks * MOE_BLOCK
    tok = jnp.broadcast_to(jnp.arange(m, dtype=jnp.int32)[:, None], (m, 2))
    slot_tok = (jnp.arange(n_slots, dtype=jnp.int32) % m).at[dest.reshape(-1)].set(tok.reshape(-1))
    block_start = jnp.arange(n_blocks, dtype=jnp.int32) * MOE_BLOCK
    block_expert = jnp.minimum(jnp.sum((pad_end[None, :] <= block_start[:, None]).astype(jnp.int32), axis=1),
                               N_EXPERTS - 1)
    n_used = (pad_end[-1:] // MOE_BLOCK).astype(jnp.int32)
    yb = _experts(h2, slot_tok, block_expert, n_used, lw["w_gate_e_b"], lw["w_up_b"], lw["w_down_b"])
    dest_tiles = dest.reshape(m // tm_comb, tm_comb, 2).transpose(0, 2, 1).reshape(-1).astype(jnp.int32)
    return _combine(yb, dest_tiles, ew, x1, mod4, t_seq, g_final, tm_comb)


def _input_projections(h1, lw, tm):
    proj = {name: _matmul(h1, lw["w_" + name], tm, 512) for name in ("q", "k", "v", "z", "xbc", "ga", "gs")}
    proj["dt"] = _dt_proj(h1, lw["w_dt"], lw["dt_bias"], tm)
    return proj


def _dt_layouts(dt_raw, bm, t, t_pad):
    hpg = HEADS_PER_GROUP
    d = dt_raw[:, :SSM_GROUPS * hpg].reshape(bm, t, SSM_GROUPS, hpg)
    d = jnp.pad(d, ((0, 0), (0, t_pad - t), (0, 0), (0, 0)))
    dt_g = jnp.pad(d.transpose(0, 2, 1, 3), ((0, 0), (0, 0), (0, 0), (0, LANES - hpg)))
    dtt_g = d.transpose(0, 2, 3, 1)
    return dt_g, dtt_g


def kernel(x_prompt, x_sample, cache_k, cache_v, state_ssm, state_conv, page_table, c_prompt, c_sample,
           w_ada, b_ada, g_mix, w_in, b_gate, conv_w, conv_b, dt_bias, a_log, d_skip, g_ssm,
           w_attn_br, w_ssm_br, w_out, g_ffn, w_group, b_group, w_router, b_router,
           w_up, w_gate_e, w_down, g_final):
    depth = w_ada.shape[0]
    assert depth == 1
    bp, t, d = x_prompt.shape
    bd, tq, _ = x_sample.shape
    att_w = N_HEADS * (d // N_HEADS)
    d_inner = g_ssm.shape[1]
    conv_dim = conv_w.shape[2]
    n_ssm_heads = dt_bias.shape[1]
    assert n_ssm_heads == SSM_GROUPS * HEADS_PER_GROUP and d_inner == n_ssm_heads * SSM_HEADDIM
    slopes = 2.0 ** (-8.0 * jnp.arange(1, N_HEADS + 1, dtype=F32) / N_HEADS)

    l = 0
    w_in_b = w_in[l].astype(BF16)
    cuts = [0, att_w, 2 * att_w, 3 * att_w, 3 * att_w + d_inner, 3 * att_w + d_inner + conv_dim,
            3 * att_w + d_inner + conv_dim + n_ssm_heads]
    cuts += [cuts[-1] + d, cuts[-1] + 2 * d]
    names = ("q", "k", "v", "z", "xbc", "dt", "ga", "gs")
    lw = {"w_" + n: w_in_b[:, cuts[i]:cuts[i + 1]] for i, n in enumerate(names)}
    lw["w_dt"] = jnp.pad(lw["w_dt"], ((0, 0), (0, LANES - n_ssm_heads)))
    lw["dt_bias"] = jnp.pad(dt_bias[l], (0, LANES - n_ssm_heads)).reshape(1, LANES)
    lw.update(w_attn_b=w_attn_br[l].astype(BF16), w_ssm_b=w_ssm_br[l].astype(BF16),
              w_out_b=w_out[l].astype(BF16), g_ffn=g_ffn[l], w_group=w_group[l], b_group=b_group[l],
              w_router=w_router[l], b_router=b_router[l], w_up_b=w_up[l].astype(BF16),
              w_gate_e_b=w_gate_e[l].astype(BF16), w_down_b=w_down[l].astype(BF16))

    mod = _ada_mod(jnp.concatenate([c_prompt, c_sample], axis=0), w_ada[l], b_ada[l])
    mod = mod.reshape(bp + bd, 6, d)
    mod_p = mod[:bp].transpose(1, 0, 2).reshape(6, bp, 1, d)
    mod_s = jnp.repeat(mod[bp:].transpose(1, 0, 2), tq, axis=1).reshape(6, 1, bd * tq, d)

    ssd_args = (conv_w[l], conv_b[l], a_log[l], d_skip[l], g_ssm[l])

    mp = bp * t
    h1 = _modnorm(x_prompt, g_mix[l], mod_p, 1, 0, 512)
    pj = _input_projections(h1, lw, 1024)
    q3, k3, v3 = (pj[n].reshape(bp, t, att_w) for n in ("q", "k", "v"))
    attn = _moba_prompt(q3, k3, v3, slopes).reshape(mp, att_w)
    xbc3 = pj["xbc"].reshape(bp, t, conv_dim)
    dt_g, dtt_g = _dt_layouts(pj["dt"], bp, t, t)
    y, ssm_p = _ssd(xbc3, pj["z"].reshape(bp, t, d_inner), dt_g, dtt_g,
                    jnp.zeros((bp, 3, conv_dim), F32), jnp.zeros((bp, d_inner, SSM_STATE), F32),
                    *ssd_args)
    merged = _merge(attn, y.reshape(mp, d_inner), lw["w_attn_b"], lw["w_ssm_b"], pj["ga"], pj["gs"],
                    b_gate[l], 512, 512)
    y_prompt = _moe_and_final_norm(merged, x_prompt.reshape(mp, d), mod_p, t, lw, g_final, 256, 128)
    y_prompt = y_prompt.reshape(bp, t, d)
    k_p = k3.reshape(1, bp, t, N_HEADS, att_w // N_HEADS)
    v_p = v3.reshape(1, bp, t, N_HEADS, att_w // N_HEADS)
    conv_p = xbc3[:, t - 3:, :][None]
    ssm_p = ssm_p.reshape(1, bp, n_ssm_heads, SSM_HEADDIM, SSM_STATE)

    ms = bd * tq
    xs3 = x_sample.reshape(1, ms, d)
    h1s = _modnorm(xs3, g_mix[l], mod_s, 1, 0, ms)
    pjs = _input_projections(h1s, lw, ms)
    qs, ks, vs = (pjs[n].reshape(bd, tq, att_w) for n in ("q", "k", "v"))
    attn_s = _moba_sample(qs, ks, vs, cache_k, cache_v, l, page_table, slopes)
    attn_s = attn_s.reshape(ms, att_w).astype(BF16)
    xbc_s = pjs["xbc"].reshape(bd, tq, conv_dim)
    pad_t = lambda a: jnp.pad(a, ((0, 0), (0, SSD_CHUNK - tq), (0, 0)))
    dt_gs, dtt_gs = _dt_layouts(pjs["dt"], bd, tq, SSD_CHUNK)
    y_s, ssm_s = _ssd(pad_t(xbc_s), pad_t(pjs["z"].reshape(bd, tq, d_inner)), dt_gs, dtt_gs,
                      state_conv[l], state_ssm[l].reshape(bd, d_inner, SSM_STATE), *ssd_args)
    y_s = y_s[:, :tq, :].reshape(ms, d_inner)
    merged_s = _merge(attn_s, y_s, lw["w_attn_b"], lw["w_ssm_b"], pjs["ga"], pjs["gs"], b_gate[l], ms, 512)
    y_sample = _moe_and_final_norm(merged_s, x_sample.reshape(ms, d), mod_s, ms, lw, g_final, ms, ms)
    y_sample = y_sample.reshape(bd, tq, d)
    k_s = ks.reshape(1, bd, tq, N_HEADS, att_w // N_HEADS)
    v_s = vs.reshape(1, bd, tq, N_HEADS, att_w // N_HEADS)
    conv_s = jnp.concatenate([state_conv[l], xbc_s], axis=1)[:, tq:, :][None]
    ssm_s = ssm_s.reshape(1, bd, n_ssm_heads, SSM_HEADDIM, SSM_STATE)

    return (y_prompt, y_sample, k_p, v_p, k_s, v_s, ssm_p, ssm_s, conv_p, conv_s)
```

```python
import functools

import jax
import jax.numpy as jnp
from jax import lax
from jax.experimental import pallas as pl
from jax.experimental.pallas import tpu as pltpu

F32 = jnp.float32
BF16 = jnp.bfloat16
NEG_INF = float("-inf")

PAGE_SIZE = 128
N_HEADS = 16
MOBA_BLOCK = 256
MOBA_TOPK = 3
SSM_HEADDIM = 64
SSM_GROUPS = 8
SSM_STATE = 128
HEADS_PER_GROUP = 8
SSD_CHUNK = 128
N_EXPERT_GROUPS = 4
EXPERTS_PER_GROUP = 8
N_EXPERTS = N_EXPERT_GROUPS * EXPERTS_PER_GROUP
MOE_BLOCK = 128
NORM_EPS = 1e-6

LANES = 128
VMEM_LIMIT_MB = 56


def _params(semantics, vmem_mb=VMEM_LIMIT_MB):
    return pltpu.CompilerParams(dimension_semantics=semantics,
                                vmem_limit_bytes=vmem_mb * 1024 * 1024)


def _sigmoid(x):
    return 0.5 * jnp.tanh(0.5 * x) + 0.5


def _silu(x):
    return x * _sigmoid(x)


def _softplus(x):
    return jnp.maximum(x, 0.0) + jnp.log1p(jnp.exp(-jnp.abs(x)))


def _dot(a, b):
    return jnp.dot(a, b, preferred_element_type=F32)


def _dot_nt(a, b):
    return lax.dot_general(a, b, (((1,), (1,)), ((), ())), preferred_element_type=F32)


def _split2(x):
    hi = x.astype(BF16)
    lo = (x - hi.astype(F32)).astype(BF16)
    return hi, lo


def _split3(x):
    p1 = x.astype(BF16)
    r1 = x - p1.astype(F32)
    p2 = r1.astype(BF16)
    p3 = (r1 - p2.astype(F32)).astype(BF16)
    return p1, p2, p3


def _dot_exact_rhs(x, e):
    p1, p2, p3 = _split3(x)
    return _dot(p1, e) + _dot(p2, e) + _dot(p3, e)


def _dot_exact_lhs(e, x):
    p1, p2, p3 = _split3(x)
    return _dot(e, p1) + _dot(e, p2) + _dot(e, p3)


def _dot_hi_nt(a, b):
    ah, al = _split2(a)
    bh, bl = _split2(b)
    return _dot_nt(ah, bh) + _dot_nt(ah, bl) + _dot_nt(al, bh)


def _ada_kernel(c_ref, w_ref, b_ref, o_ref):
    a = _silu(c_ref[...]).astype(BF16)
    o_ref[...] = _dot(a, w_ref[...].astype(BF16)) + b_ref[...]


def _ada_mod(c, w_ada, b_ada):
    mc, d = c.shape
    n = w_ada.shape[1]
    tn = 1024
    return pl.pallas_call(
        _ada_kernel,
        grid=(n // tn,),
        in_specs=[pl.BlockSpec((mc, d), lambda j: (0, 0)),
                  pl.BlockSpec((d, tn), lambda j: (0, j)),
                  pl.BlockSpec((1, tn), lambda j: (0, j))],
        out_specs=pl.BlockSpec((mc, tn), lambda j: (0, j)),
        out_shape=jax.ShapeDtypeStruct((mc, n), F32),
        compiler_params=_params(("parallel",)),
        name="ada_mod",
    )(c, w_ada, b_ada.reshape(1, n))


def _modnorm_kernel(x_ref, g_ref, sc_ref, sh_ref, o_ref):
    x = x_ref[0]
    xn = x * lax.rsqrt(jnp.mean(x * x, axis=-1, keepdims=True) + NORM_EPS)
    h = (xn * g_ref[...]) * (1.0 + sc_ref[0, 0]) + sh_ref[0, 0]
    o_ref[...] = h.astype(o_ref.dtype)


def _modnorm(x3, g, mod4, scale_idx, shift_idx, tt):
    bm, t, d = x3.shape
    r = mod4.shape[2]
    assert r == 1 or (r == t and tt == t)
    nt = t // tt
    return pl.pallas_call(
        _modnorm_kernel,
        grid=(bm, nt),
        in_specs=[pl.BlockSpec((1, tt, d), lambda b, i: (b, i, 0)),
                  pl.BlockSpec((1, d), lambda b, i: (0, 0)),
                  pl.BlockSpec((1, 1, r, d), lambda b, i: (scale_idx, b, 0, 0)),
                  pl.BlockSpec((1, 1, r, d), lambda b, i: (shift_idx, b, 0, 0))],
        out_specs=pl.BlockSpec((tt, d), lambda b, i: (b * nt + i, 0)),
        out_shape=jax.ShapeDtypeStruct((bm * t, d), BF16),
        compiler_params=_params(("parallel", "parallel")),
        name="modnorm",
    )(x3, g.reshape(1, d), mod4, mod4)


def _mm_kernel(a_ref, w_ref, o_ref):
    o_ref[...] = _dot(a_ref[...], w_ref[...]).astype(o_ref.dtype)


def _matmul(a, w, tm, tn, out_dtype=F32):
    m, k = a.shape
    n = w.shape[1]
    tm = min(tm, m)
    tn = min(tn, n)
    return pl.pallas_call(
        _mm_kernel,
        grid=(m // tm, n // tn),
        in_specs=[pl.BlockSpec((tm, k), lambda i, j: (i, 0)),
                  pl.BlockSpec((k, tn), lambda i, j: (0, j))],
        out_specs=pl.BlockSpec((tm, tn), lambda i, j: (i, j)),
        out_shape=jax.ShapeDtypeStruct((m, n), out_dtype),
        compiler_params=_params(("parallel", "parallel")),
        name="matmul",
    )(a, w)


def _mm_wcast_kernel(a_ref, w_ref, o_ref, wb_ref):
    @pl.when(pl.program_id(1) == 0)
    def _():
        wb_ref[...] = w_ref[0].astype(BF16)

    o_ref[...] = _dot(a_ref[...], wb_ref[...])


def _matmul_wcast(a, w3, layer, col0, n, tm, tn):
    m, k = a.shape
    tm = min(tm, m)
    assert col0 % tn == 0 and n % tn == 0
    c0 = col0 // tn
    return pl.pallas_call(
        _mm_wcast_kernel,
        grid=(n // tn, m // tm),
        in_specs=[pl.BlockSpec((tm, k), lambda j, i: (i, 0)),
                  pl.BlockSpec((1, k, tn), lambda j, i: (layer, 0, c0 + j))],
        out_specs=pl.BlockSpec((tm, tn), lambda j, i: (i, j)),
        out_shape=jax.ShapeDtypeStruct((m, n), F32),
        scratch_shapes=[pltpu.VMEM((k, tn), BF16)],
        compiler_params=_params(("parallel", "arbitrary")),
        name="matmul_wcast",
    )(a, w3)


def _dt_kernel(a_ref, w_ref, b_ref, o_ref):
    o_ref[...] = _softplus(_dot(a_ref[...], w_ref[...]) + b_ref[...])


def _dt_proj(a, w, bias, tm):
    m, k = a.shape
    n = w.shape[1]
    tm = min(tm, m)
    return pl.pallas_call(
        _dt_kernel, grid=(m // tm,),
        in_specs=[pl.BlockSpec((tm, k), lambda i: (i, 0)),
                  pl.BlockSpec((k, n), lambda i: (0, 0)),
                  pl.BlockSpec((1, n), lambda i: (0, 0))],
        out_specs=pl.BlockSpec((tm, n), lambda i: (i, 0)),
        out_shape=jax.ShapeDtypeStruct((m, n), F32),
        compiler_params=_params(("parallel",)),
        name="dt_proj",
    )(a, w, bias)


def _moba_prompt_kernel(slopes_ref, q_ref, k_ref, v_ref, o_ref,
                        kb_ref, vb_ref, km_ref, bias_ref, bias_own_ref, s_ref, pen_ref, *, n_blk, scale):
    h = pl.program_id(1)
    ob = pl.program_id(2)
    blk = MOBA_BLOCK
    slope = slopes_ref[h]

    @pl.when(ob == 0)
    def _():
        k = k_ref[0]
        kb_ref[...] = k.astype(BF16)
        vb_ref[...] = v_ref[0].astype(BF16)
        for j in range(n_blk):
            km_ref[j:j + 1, :] = jnp.sum(k[j * blk:(j + 1) * blk], axis=0, keepdims=True) * (1.0 / blk)
        row = lax.broadcasted_iota(jnp.int32, (blk, blk), 0)
        col = lax.broadcasted_iota(jnp.int32, (blk, blk), 1)
        b0 = slope * (row - col).astype(F32)
        bias_ref[...] = b0
        bias_own_ref[...] = jnp.where(col <= row, b0, float("inf"))
        t = k.shape[0]
        gt = _dot_hi_nt(km_ref[...], q_ref[0])
        jrow = lax.broadcasted_iota(jnp.int32, gt.shape, 0)
        own = lax.broadcasted_iota(jnp.int32, gt.shape, 1) // blk
        gt = jnp.where(jrow < own, gt, NEG_INF)
        pen_rows = []
        for j in range(n_blk):
            gj = gt[j:j + 1, :]
            beats = (gt > gj) | ((gt == gj) & (jrow < j))
            rank = jnp.sum(jnp.where(beats, 1.0, 0.0), axis=0, keepdims=True)
            pen_rows.append(jnp.where(rank < MOBA_TOPK, 0.0, NEG_INF))
        pen_rows.append(jnp.zeros((LANES - n_blk, t), F32))
        pen_ref[...] = jnp.concatenate(pen_rows, axis=0).T

    q = q_ref[0, pl.ds(pl.multiple_of(ob * blk, blk), blk), :]
    qb = q.astype(BF16)
    pen = pen_ref[pl.ds(pl.multiple_of(ob * blk, blk), blk), :]

    def attend(n):
        mx = None
        for jj in range(n + 1):
            qk = _dot_nt(qb, kb_ref[jj * blk:(jj + 1) * blk, :]) * scale
            if jj == n:
                s = qk - bias_own_ref[...]
            else:
                s = (qk - bias_ref[...]) + (pen[:, jj:jj + 1] - slope * float((n - jj) * blk))
            s_ref[jj] = s
            part = jnp.maximum(s[:, :LANES], s[:, LANES:])
            mx = part if mx is None else jnp.maximum(mx, part)
        m = jnp.max(mx, axis=1, keepdims=True)
        lsum = None
        acc = None
        for jj in range(n + 1):
            p = jnp.exp(s_ref[jj] - m)
            part = p[:, :LANES] + p[:, LANES:]
            lsum = part if lsum is None else lsum + part
            pv = _dot(p.astype(BF16), vb_ref[jj * blk:(jj + 1) * blk, :])
            acc = pv if acc is None else acc + pv
        l = jnp.sum(lsum, axis=1, keepdims=True)
        o_ref[0] = (acc / l).astype(o_ref.dtype)

    for n in range(n_blk):
        pl.when(ob == n)(functools.partial(attend, n))


def _moba_prompt(q3, k3, v3, slopes):
    b, t, w = q3.shape
    hd = w // N_HEADS
    assert t % MOBA_BLOCK == 0 and hd == LANES
    n_blk = t // MOBA_BLOCK
    assert n_blk >= MOBA_TOPK
    kern = functools.partial(_moba_prompt_kernel, n_blk=n_blk, scale=hd ** -0.5)
    grid_spec = pltpu.PrefetchScalarGridSpec(
        num_scalar_prefetch=1,
        grid=(b, N_HEADS, n_blk),
        in_specs=[pl.BlockSpec((1, t, hd), lambda bi, h, i, s: (bi, 0, h)),
                  pl.BlockSpec((1, t, hd), lambda bi, h, i, s: (bi, 0, h)),
                  pl.BlockSpec((1, t, hd), lambda bi, h, i, s: (bi, 0, h))],
        out_specs=pl.BlockSpec((1, MOBA_BLOCK, hd), lambda bi, h, i, s: (bi, i, h)),
        scratch_shapes=[pltpu.VMEM((t, hd), BF16), pltpu.VMEM((t, hd), BF16),
                        pltpu.VMEM((n_blk, hd), F32),
                        pltpu.VMEM((MOBA_BLOCK, MOBA_BLOCK), F32), pltpu.VMEM((MOBA_BLOCK, MOBA_BLOCK), F32),
                        pltpu.VMEM((n_blk, MOBA_BLOCK, MOBA_BLOCK), F32),
                        pltpu.VMEM((t, LANES), F32)])
    return pl.pallas_call(
        kern, grid_spec=grid_spec,
        out_shape=jax.ShapeDtypeStruct((b, t, w), BF16),
        compiler_params=_params(("parallel", "parallel", "arbitrary")),
        name="moba_prompt",
    )(slopes, q3, k3, v3)


BLOCKS_PER_MEAN_STEP = 4


def _kmean_kernel(pt_ref, *refs):
    page_refs, o_ref = refs[:-1], refs[-1]
    ppb = MOBA_BLOCK // PAGE_SIZE
    for i in range(BLOCKS_PER_MEAN_STEP):
        s = jnp.sum(page_refs[ppb * i][0, 0], axis=0)
        for p in range(1, ppb):
            s = s + jnp.sum(page_refs[ppb * i + p][0, 0], axis=0)
        o_ref[0, i] = s * (1.0 / MOBA_BLOCK)


def _cache_block_means(cache_k5, layer, page_table):
    bd, n_pages = page_table.shape
    _, _, page, nh, hd = cache_k5.shape
    ppb = MOBA_BLOCK // page
    n_full = n_pages // ppb
    assert n_full % BLOCKS_PER_MEAN_STEP == 0
    pps = ppb * BLOCKS_PER_MEAN_STEP

    def page_spec(i):
        return pl.BlockSpec((1, 1, page, nh, hd),
                            lambda b, j, pt, i=i: (layer, pt[b * n_pages + pps * j + i], 0, 0, 0))

    grid_spec = pltpu.PrefetchScalarGridSpec(
        num_scalar_prefetch=1,
        grid=(bd, n_full // BLOCKS_PER_MEAN_STEP),
        in_specs=[page_spec(i) for i in range(pps)],
        out_specs=pl.BlockSpec((1, BLOCKS_PER_MEAN_STEP, nh, hd), lambda b, j, pt: (b, j, 0, 0)))
    return pl.pallas_call(
        _kmean_kernel, grid_spec=grid_spec,
        out_shape=jax.ShapeDtypeStruct((bd, n_full, nh, hd), F32),
        compiler_params=_params(("parallel", "parallel")),
        name="cache_block_means",
    )(page_table.reshape(-1), *([cache_k5] * pps))


def _sample_topk_kernel(q_ref, km_ref, o_ref, *, n_full):
    q = q_ref[0]
    rows = q.shape[0]
    lane = lax.broadcasted_iota(jnp.int32, (rows, n_full), 1).astype(F32)
    lane_o = lax.broadcasted_iota(jnp.int32, (rows, LANES), 1)
    for h in range(N_HEADS):
        g = _dot_hi_nt(q[:, h * LANES:(h + 1) * LANES], km_ref[0, :, h, :])
        res = jnp.zeros((rows, LANES), F32)
        for s in range(MOBA_TOPK):
            mx = jnp.max(g, axis=1, keepdims=True)
            idx = jnp.min(jnp.where(g == mx, lane, float(n_full)), axis=1, keepdims=True)
            res = jnp.where(lane_o == s, idx, res)
            g = jnp.where(lane == idx, NEG_INF, g)
        o_ref[0, h * rows:(h + 1) * rows, :] = res.astype(jnp.int32)


def _sample_topk(q_pad, kmean):
    bd, rows, w = q_pad.shape
    _, n_full, nh, hd = kmean.shape
    assert n_full >= MOBA_TOPK and nh == N_HEADS and hd == LANES
    kern = functools.partial(_sample_topk_kernel, n_full=n_full)
    return pl.pallas_call(
        kern, grid=(bd,),
        in_specs=[pl.BlockSpec((1, rows, w), lambda b: (b, 0, 0)),
                  pl.BlockSpec((1, n_full, nh, hd), lambda b: (b, 0, 0, 0))],
        out_specs=pl.BlockSpec((1, N_HEADS * rows, LANES), lambda b: (b, 0, 0)),
        out_shape=jax.ShapeDtypeStruct((bd, N_HEADS * rows, LANES), jnp.int32),
        compiler_params=_params(("parallel",)),
        name="sample_topk",
    )(q_pad, kmean)


def _sample_attn_kernel(pt_ref, sel_ref, slopes_ref, q_ref, kn_ref, vn_ref, *rest,
                        tq, past, n_full, scale):
    ppb = MOBA_BLOCK // PAGE_SIZE
    k_refs = rest[:ppb]
    v_refs = rest[ppb:2 * ppb]
    o_ref, qs_ref, bias_ref, m_ref, l_ref, acc_ref = rest[2 * ppb:]
    b = pl.program_id(0)
    j = pl.program_id(1)
    rows = q_ref.shape[1]
    n_rows = N_HEADS * rows
    n_keys = MOBA_BLOCK * N_HEADS
    head_of_row = lax.broadcasted_iota(jnp.int32, (n_rows, 1), 0) // rows
    q_of_row = lax.broadcasted_iota(jnp.int32, (n_rows, 1), 0) % rows
    slope_col = jnp.zeros((n_rows, 1), F32)
    for h in range(N_HEADS):
        slope_col = jnp.where(head_of_row == h, slopes_ref[h], slope_col)

    @pl.when(j == 0)
    def _():
        q = q_ref[0]
        r4 = lax.broadcasted_iota(jnp.int32, (rows, tq), 0)
        c4 = lax.broadcasted_iota(jnp.int32, (rows, tq), 1)
        for h in range(N_HEADS):
            sl = slice(h * LANES, (h + 1) * LANES)
            hs = slice(h * rows, (h + 1) * rows)
            qs_ref[hs, :] = q[:, sl].astype(BF16)
            kn = kn_ref[0, :, sl]
            vn = vn_ref[0, :, sl]
            s = _dot_nt(q[:, sl].astype(BF16), kn.astype(BF16)) * scale - slopes_ref[h] * (r4 - c4).astype(F32)
            s = jnp.where(c4 <= r4, s, NEG_INF)
            m = jnp.max(s, axis=1, keepdims=True)
            p = jnp.exp(s - m)
            acc = p[:, 0:1] * vn[0:1, :]
            for c in range(1, tq):
                acc = acc + p[:, c:c + 1] * vn[c:c + 1, :]
            m_ref[hs, :] = m
            l_ref[hs, :] = jnp.sum(p, axis=1, keepdims=True)
            acc_ref[hs, :] = acc
        key = lax.broadcasted_iota(jnp.int32, (n_rows, n_keys), 1)
        rel = (q_of_row - key // N_HEADS).astype(F32)
        bias_ref[...] = jnp.where(key % N_HEADS == head_of_row, slope_col * rel, float("inf"))

    bits_col = jnp.zeros((n_rows, 1), jnp.int32)
    for h in range(N_HEADS):
        bits_col = jnp.where(head_of_row == h, sel_ref[(b * N_HEADS + h) * n_full + j], bits_col)
    picked = ((bits_col >> q_of_row) & 1) == 1
    offset = (past - j * MOBA_BLOCK).astype(F32)
    t_col = jnp.where(picked, -(slope_col * offset), NEG_INF)

    kp = jnp.concatenate([r[0, 0].reshape(PAGE_SIZE * N_HEADS, LANES) for r in k_refs], axis=0).astype(BF16)
    vp = jnp.concatenate([r[0, 0].reshape(PAGE_SIZE * N_HEADS, LANES) for r in v_refs], axis=0).astype(BF16)
    s = _dot_nt(qs_ref[...], kp) * scale - bias_ref[...]
    m_old = m_ref[...]
    m_new = jnp.maximum(m_old, jnp.max(s, axis=1, keepdims=True) + t_col)
    alpha = jnp.exp(m_old - m_new)
    p = jnp.exp(s - (m_new - t_col))
    l_ref[...] = alpha * l_ref[...] + jnp.sum(p, axis=1, keepdims=True)
    acc_ref[...] = alpha * acc_ref[...] + _dot(p.astype(BF16), vp)
    m_ref[...] = m_new

    @pl.when(j == n_full - 1)
    def _():
        for h in range(N_HEADS):
            hs = slice(h * rows, (h + 1) * rows)
            o_ref[0, :, h * LANES:(h + 1) * LANES] = acc_ref[hs, :] / l_ref[hs, :]


def _sample_attention(q_pad, k_new, v_new, cache_k5, cache_v5, layer, page_table, sel_bits, slopes):
    bd, rows, w = q_pad.shape
    tq = k_new.shape[1]
    n_pages = page_table.shape[1]
    _, _, page, nh, hd = cache_k5.shape
    ppb = MOBA_BLOCK // page
    n_full = n_pages // ppb
    assert nh == N_HEADS and hd == LANES and n_pages % ppb == 0

    def page_spec(i):
        return pl.BlockSpec((1, 1, page, nh, hd),
                            lambda b, j, pt, sel, sl, i=i: (layer, pt[b * n_pages + ppb * j + i], 0, 0, 0))

    grid_spec = pltpu.PrefetchScalarGridSpec(
        num_scalar_prefetch=3,
        grid=(bd, n_full),
        in_specs=[pl.BlockSpec((1, rows, w), lambda b, j, *_: (b, 0, 0)),
                  pl.BlockSpec((1, tq, w), lambda b, j, *_: (b, 0, 0)),
                  pl.BlockSpec((1, tq, w), lambda b, j, *_: (b, 0, 0))]
        + [page_spec(i) for i in range(ppb)] + [page_spec(i) for i in range(ppb)],
        out_specs=pl.BlockSpec((1, rows, w), lambda b, j, *_: (b, 0, 0)),
        scratch_shapes=[pltpu.VMEM((N_HEADS * rows, hd), BF16),
                        pltpu.VMEM((N_HEADS * rows, MOBA_BLOCK * N_HEADS), F32),
                        pltpu.VMEM((N_HEADS * rows, 1), F32), pltpu.VMEM((N_HEADS * rows, 1), F32),
                        pltpu.VMEM((N_HEADS * rows, hd), F32)])
    kern = functools.partial(_sample_attn_kernel, tq=tq, past=n_pages * page, n_full=n_full, scale=hd ** -0.5)
    return pl.pallas_call(
        kern, grid_spec=grid_spec,
        out_shape=jax.ShapeDtypeStruct((bd, rows, w), F32),
        compiler_params=_params(("parallel", "arbitrary")),
        name="sample_attention",
    )(page_table.reshape(-1), sel_bits, slopes, q_pad, k_new, v_new,
      *([cache_k5] * ppb), *([cache_v5] * ppb))


def _moba_sample(qs, ks, vs, cache_k5, cache_v5, layer, page_table, slopes):
    bd, tq, w = qs.shape
    n_pages = page_table.shape[1]
    assert (n_pages * PAGE_SIZE) % MOBA_BLOCK == 0 and n_pages > 0 and tq <= 8
    n_full = n_pages * PAGE_SIZE // MOBA_BLOCK
    kmean = _cache_block_means(cache_k5, layer, page_table)
    q_pad = jnp.pad(qs, ((0, 0), (0, 8 - tq), (0, 0)))
    top = _sample_topk(q_pad, kmean).reshape(bd, N_HEADS, 8, LANES)[:, :, :tq, :MOBA_TOPK]
    hit = jnp.any(top[..., None] == jnp.arange(n_full, dtype=jnp.int32), axis=3)
    sel_bits = jnp.sum(jnp.where(hit, 1 << jnp.arange(tq, dtype=jnp.int32)[:, None], 0), axis=2)
    out = _sample_attention(q_pad, ks, vs, cache_k5, cache_v5, layer, page_table,
                            sel_bits.reshape(-1).astype(jnp.int32), slopes)
    return out[:, :tq, :]


def _ssd_kernel(x_ref, b_ref, c_ref, wx_ref, wb_ref, wc_ref, bx_ref, bb_ref, bc_ref,
                ix_ref, ib_ref, ic_ref, dt_ref, dtt_ref,
                alog_row_ref, alog_col_ref, dskip_ref, z_ref, gssm_ref, s0_ref,
                y_ref, sfin_ref, carry_ref, cbuf_ref, ubuf_ref, *, n_groups, d_inner):
    L = SSD_CHUNK
    gw = x_ref.shape[2]
    n_state = b_ref.shape[2]
    hpg = gw // SSM_HEADDIM
    cw = gw + 2 * n_state
    c = pl.program_id(1)
    g = pl.program_id(2)
    rows_g = pl.ds(pl.multiple_of(g * gw, gw), gw)

    @pl.when(c == 0)
    def _():
        carry_ref[g] = jnp.zeros((8, cw), F32)
        carry_ref[g, 5:8, 0:gw] = ix_ref[0]
        carry_ref[g, 5:8, gw:gw + n_state] = ib_ref[0]
        carry_ref[g, 5:8, gw + n_state:cw] = ic_ref[0]
        sfin_ref[0, rows_g, :] = s0_ref[0, rows_g, :]

    cbuf_ref[0:8, :] = carry_ref[g]
    cbuf_ref[8:8 + L, 0:gw] = x_ref[0]
    cbuf_ref[8:8 + L, gw:gw + n_state] = b_ref[0]
    cbuf_ref[8:8 + L, gw + n_state:cw] = c_ref[0]
    carry_ref[g] = cbuf_ref[L:L + 8, :]
    w = jnp.concatenate([wx_ref[...], wb_ref[...], wc_ref[...]], axis=1)
    bias = jnp.concatenate([bx_ref[...], bb_ref[...], bc_ref[...]], axis=1)
    window = cbuf_ref[...]
    conv = bias + w[3:4] * window[8:8 + L]
    for k in range(1, 4):
        conv = conv + w[3 - k:4 - k] * pltpu.roll(window, k, 0)[8:8 + L]
    act = _silu(conv)
    xs = act[:, 0:gw]
    bm = act[:, gw:gw + n_state].astype(BF16)
    cm = act[:, gw + n_state:cw].astype(BF16)

    dt = dt_ref[0, 0]
    dtt = dtt_ref[0, 0]
    adt = dt * (-jnp.exp(alog_row_ref[0]))
    adtt = dtt * (-jnp.exp(alog_col_ref[0]))

    row = lax.broadcasted_iota(jnp.int32, (L, L), 0)
    col = lax.broadcasted_iota(jnp.int32, (L, L), 1)
    causal = col <= row
    tri_l = jnp.where(causal, 1.0, 0.0).astype(BF16)
    tri_u = jnp.where(row <= col, 1.0, 0.0).astype(BF16)
    acs_col = _dot_exact_lhs(tri_l, adt)
    acs_row = _dot_exact_rhs(adtt, tri_u)

    er = lax.broadcasted_iota(jnp.int32, (LANES, gw), 0)
    ec = lax.broadcasted_iota(jnp.int32, (LANES, gw), 1)
    expand = jnp.where(er == ec // SSM_HEADDIM, 1.0, 0.0).astype(BF16)
    eacs = jnp.exp(acs_col)
    dte = jnp.exp(acs_col[L - 1:L, :] - acs_col)
    st_hi, st_lo = _split2(jnp.concatenate([dt, eacs, dte], axis=0))
    stacked = _dot(st_hi, expand) + _dot(st_lo, expand)
    dt_x = stacked[0:L]
    eacs_x = stacked[L:2 * L]
    dte_x = stacked[2 * L:3 * L]

    xdt = xs * dt_x
    xdt_b = xdt.astype(BF16)
    cb = _dot_nt(cm, bm)
    lane = lax.broadcasted_iota(jnp.int32, (L, LANES), 1)
    pairs = []
    for i in range(hpg // 2):
        xp = xdt_b[:, i * LANES:(i + 1) * LANES]
        res = []
        for j in (2 * i, 2 * i + 1):
            diff = acs_col[:, j:j + 1] - acs_row[j:j + 1, :]
            decay = jnp.exp(jnp.where(causal, diff, NEG_INF))
            res.append(_dot((cb * decay).astype(BF16), xp))
        pairs.append(jnp.where(lane < SSM_HEADDIM, res[0], res[1]))
    y_diag = jnp.concatenate(pairs, axis=1)

    s_prev = sfin_ref[0, rows_g, :]
    y_off = _dot_nt(cm, s_prev.astype(BF16)) * eacs_x
    y = y_diag + y_off + dskip_ref[...] * xs
    ubuf_ref[g] = y * _silu(z_ref[0])

    xw_t = (xdt * dte_x).T.astype(BF16)
    s_chunk = _dot(xw_t, bm)
    last = jnp.broadcast_to(acs_row[:, L - 1:L], (hpg, n_state))
    last = jnp.concatenate([last, jnp.zeros((LANES - hpg, n_state), F32)], axis=0)
    tr = lax.broadcasted_iota(jnp.int32, (gw, LANES), 0)
    tc = lax.broadcasted_iota(jnp.int32, (gw, LANES), 1)
    expand_t = jnp.where(tc == tr // SSM_HEADDIM, 1.0, 0.0).astype(BF16)
    chunk_decay = jnp.exp(_dot_exact_lhs(expand_t, last))
    sfin_ref[0, rows_g, :] = chunk_decay * s_prev + s_chunk

    @pl.when(g == n_groups - 1)
    def _():
        ss = jnp.zeros((L, 1), F32)
        for gi in range(n_groups):
            u = ubuf_ref[gi]
            ss = ss + jnp.sum(u * u, axis=1, keepdims=True)
        r = lax.rsqrt(ss * (1.0 / d_inner) + NORM_EPS)
        for gi in range(n_groups):
            yn = (ubuf_ref[gi] * r) * gssm_ref[:, gi * gw:(gi + 1) * gw]
            y_ref[0, :, gi * gw:(gi + 1) * gw] = yn.astype(y_ref.dtype)


def _ssd(xbc3, z3, dt_g, dtt_g, conv_init, ssm_init, conv_w, conv_b, a_log, d_skip, g_ssm):
    bm, t, conv_dim = xbc3.shape
    d_inner = z3.shape[2]
    G = SSM_GROUPS
    n_state = SSM_STATE
    gw = d_inner // G
    hpg = gw // SSM_HEADDIM
    assert hpg % 2 == 0 and hpg <= 8 and n_state == LANES and 2 * SSM_HEADDIM == LANES
    L = SSD_CHUNK
    assert t % L == 0
    nc = t // L
    cw = gw + 2 * n_state
    xb = d_inner // n_state
    cbk = xb + G
    pad = lambda a: jnp.pad(a, ((0, 0), (0, 0), (0, LANES - a.shape[2])))
    alog_row = pad(a_log.reshape(G, 1, hpg))
    alog_col = a_log.reshape(G, hpg, 1)
    dskip_x = jnp.repeat(d_skip, SSM_HEADDIM).reshape(1, d_inner)
    conv_b2 = conv_b.reshape(1, conv_dim)
    kern = functools.partial(_ssd_kernel, n_groups=G, d_inner=d_inner)
    in_specs = [
        pl.BlockSpec((1, L, gw), lambda b, c, g: (b, c, g)),
        pl.BlockSpec((1, L, n_state), lambda b, c, g: (b, c, xb + g)),
        pl.BlockSpec((1, L, n_state), lambda b, c, g: (b, c, cbk + g)),
        pl.BlockSpec((4, gw), lambda b, c, g: (0, g)),
        pl.BlockSpec((4, n_state), lambda b, c, g: (0, xb + g)),
        pl.BlockSpec((4, n_state), lambda b, c, g: (0, cbk + g)),
        pl.BlockSpec((1, gw), lambda b, c, g: (0, g)),
        pl.BlockSpec((1, n_state), lambda b, c, g: (0, xb + g)),
        pl.BlockSpec((1, n_state), lambda b, c, g: (0, cbk + g)),
        pl.BlockSpec((1, 3, gw), lambda b, c, g: (b, 0, g)),
        pl.BlockSpec((1, 3, n_state), lambda b, c, g: (b, 0, xb + g)),
        pl.BlockSpec((1, 3, n_state), lambda b, c, g: (b, 0, cbk + g)),
        pl.BlockSpec((1, 1, L, LANES), lambda b, c, g: (b, g, c, 0)),
        pl.BlockSpec((1, 1, hpg, L), lambda b, c, g: (b, g, 0, c)),
        pl.BlockSpec((1, 1, LANES), lambda b, c, g: (g, 0, 0)),
        pl.BlockSpec((1, hpg, 1), lambda b, c, g: (g, 0, 0)),
        pl.BlockSpec((1, gw), lambda b, c, g: (0, g)),
        pl.BlockSpec((1, L, gw), lambda b, c, g: (b, c, g)),
        pl.BlockSpec((1, d_inner), lambda b, c, g: (0, 0)),
        pl.BlockSpec((1, d_inner, n_state), lambda b, c, g: (b, 0, 0)),
    ]
    out_specs = [pl.BlockSpec((1, L, d_inner), lambda b, c, g: (b, c, 0)),
                 pl.BlockSpec((1, d_inner, n_state), lambda b, c, g: (b, 0, 0))]
    return pl.pallas_call(
        kern, grid=(bm, nc, G), in_specs=in_specs, out_specs=out_specs,
        out_shape=[jax.ShapeDtypeStruct((bm, t, d_inner), BF16),
                   jax.ShapeDtypeStruct((bm, d_inner, n_state), F32)],
        scratch_shapes=[pltpu.VMEM((G, 8, cw), F32), pltpu.VMEM((8 + L, cw), F32),
                        pltpu.VMEM((G, L, gw), F32)],
        compiler_params=_params(("parallel", "arbitrary", "arbitrary")),
        name="ssd",
    )(xbc3, xbc3, xbc3, conv_w, conv_w, conv_w, conv_b2, conv_b2, conv_b2,
      conv_init, conv_init, conv_init, dt_g, dtt_g, alog_row, alog_col,
      dskip_x, z3, g_ssm.reshape(1, d_inner), ssm_init)


def _merge_kernel(a_ref, y_ref, wa_ref, ws_ref, ga_ref, gs_ref, ba_ref, bs_ref, o_ref):
    pa = _dot(a_ref[...], wa_ref[...])
    ps = _dot(y_ref[...], ws_ref[...])
    o = _sigmoid(ga_ref[...] + ba_ref[...]) * pa + _sigmoid(gs_ref[...] + bs_ref[...]) * ps
    o_ref[...] = o.astype(o_ref.dtype)


def _merge(attn, y, w_a, w_s, ga, gs, b_gate, tm, tn):
    m, ka = attn.shape
    ks = y.shape[1]
    n = w_a.shape[1]
    tm = min(tm, m)
    ba = b_gate[:n].reshape(1, n)
    bs = b_gate[n:].reshape(1, n)
    return pl.pallas_call(
        _merge_kernel, grid=(m // tm, n // tn),
        in_specs=[pl.BlockSpec((tm, ka), lambda i, j: (i, 0)),
                  pl.BlockSpec((tm, ks), lambda i, j: (i, 0)),
                  pl.BlockSpec((ka, tn), lambda i, j: (0, j)),
                  pl.BlockSpec((ks, tn), lambda i, j: (0, j)),
                  pl.BlockSpec((tm, tn), lambda i, j: (i, j)),
                  pl.BlockSpec((tm, tn), lambda i, j: (i, j)),
                  pl.BlockSpec((1, tn), lambda i, j: (0, j)),
                  pl.BlockSpec((1, tn), lambda i, j: (0, j))],
        out_specs=pl.BlockSpec((tm, tn), lambda i, j: (i, j)),
        out_shape=jax.ShapeDtypeStruct((m, n), BF16),
        compiler_params=_params(("parallel", "parallel")),
        name="merge",
    )(attn, y, w_a, w_s, ga, gs, ba, bs)


def _outproj_route_kernel(mg_ref, wout_ref, x_ref, g1_ref, sc2_ref, sh2_ref, gffn_ref,
                          wgh_ref, wgl_ref, bg_ref, wrh_ref, wrl_ref, br_ref,
                          x1_ref, h2_ref, ids_ref, ew_ref, cnt_ref, carry_ref):
    i = pl.program_id(0)

    @pl.when(i == 0)
    def _():
        carry_ref[...] = jnp.zeros_like(carry_ref)

    x1 = x_ref[...] + g1_ref[0, 0] * _dot(mg_ref[...], wout_ref[...])
    x1_ref[...] = x1
    xn = x1 * lax.rsqrt(jnp.mean(x1 * x1, axis=-1, keepdims=True) + NORM_EPS)
    h2 = (xn * gffn_ref[...]) * (1.0 + sc2_ref[0, 0]) + sh2_ref[0, 0]
    h2_ref[...] = h2

    hh, hl = _split2(h2)
    lg = _dot(hh, wgh_ref[...]) + _dot(hh, wgl_ref[...]) + _dot(hl, wgh_ref[...]) + bg_ref[...]
    le = _dot(hh, wrh_ref[...]) + _dot(hh, wrl_ref[...]) + _dot(hl, wrh_ref[...]) + br_ref[...]
    tm = lg.shape[0]
    lane_i = lax.broadcasted_iota(jnp.int32, (tm, LANES), 1)
    lane = lane_i.astype(F32)
    in_g = lane_i < N_EXPERT_GROUPS
    lgm = jnp.where(in_g, lg, NEG_INF)
    mg = jnp.max(lgm, axis=1, keepdims=True)
    gtop = jnp.min(jnp.where(lgm == mg, lane, float(LANES)), axis=1, keepdims=True)
    gw = 1.0 / jnp.sum(jnp.where(in_g, jnp.exp(lg - mg), 0.0), axis=1, keepdims=True)
    grp = (lane_i // EXPERTS_PER_GROUP).astype(F32)
    lem = jnp.where((lane_i < N_EXPERTS) & (grp == gtop), le, NEG_INF)
    e1 = jnp.max(lem, axis=1, keepdims=True)
    i1 = jnp.min(jnp.where(lem == e1, lane, float(LANES)), axis=1, keepdims=True)
    lem2 = jnp.where(lane == i1, NEG_INF, lem)
    e2 = jnp.max(lem2, axis=1, keepdims=True)
    i2 = jnp.min(jnp.where(lem2 == e2, lane, float(LANES)), axis=1, keepdims=True)
    t2 = jnp.exp(e2 - e1)
    den = 1.0 + t2
    w1 = (1.0 / den) * gw
    w2 = (t2 / den) * gw

    hit1 = lane == i1
    hit2 = lane == i2
    onehot = jnp.where(hit1 | hit2, 1.0, 0.0)
    rr = lax.broadcasted_iota(jnp.int32, (tm, tm), 0)
    cc = lax.broadcasted_iota(jnp.int32, (tm, tm), 1)
    strict = jnp.where(cc < rr, 1.0, 0.0).astype(BF16)
    cum = _dot(strict, onehot.astype(BF16)) + carry_ref[...]
    r1 = jnp.sum(jnp.where(hit1, cum, 0.0), axis=1, keepdims=True)
    r2 = jnp.sum(jnp.where(hit2, cum, 0.0), axis=1, keepdims=True)
    carry_ref[...] = carry_ref[...] + jnp.sum(onehot, axis=0, keepdims=True)
    cnt_ref[...] = carry_ref[...]
    packed = jnp.where(lane_i == 0, i1, jnp.where(lane_i == 1, i2,
                       jnp.where(lane_i == 2, r1, jnp.where(lane_i == 3, r2, 0.0))))
    ids_ref[...] = packed.astype(jnp.int32)
    ew_ref[...] = jnp.where(lane_i == 0, w1, jnp.where(lane_i == 1, w2, 0.0))


def _outproj_route(merged, w_out, x2, mod4, t_seq, g_ffn, w_group, b_group, w_router, b_router, tm):
    m, d = x2.shape
    r = mod4.shape[2]
    assert t_seq % tm == 0 and (r == 1 or tm == t_seq)
    per_seq = t_seq // tm

    def pad_cols(w):
        return jnp.pad(w, ((0, 0), (0, LANES - w.shape[1])))

    wg = pad_cols(w_group)
    wr = pad_cols(w_router)
    wgh, wgl = _split2(wg)
    wrh, wrl = _split2(wr)
    bg = pad_cols(b_group.reshape(1, -1))
    br = pad_cols(b_router.reshape(1, -1))
    row_spec = pl.BlockSpec((tm, d), lambda i: (i, 0))
    full = lambda shape: pl.BlockSpec(shape, lambda i: tuple(0 for _ in shape))
    mod_spec = lambda idx: pl.BlockSpec((1, 1, r, d), lambda i: (idx, i // per_seq, 0, 0))
    lane_spec = pl.BlockSpec((tm, LANES), lambda i: (i, 0))
    return pl.pallas_call(
        _outproj_route_kernel, grid=(m // tm,),
        in_specs=[row_spec, full((d, d)), row_spec, mod_spec(2), mod_spec(4), mod_spec(3), full((1, d)),
                  full((d, LANES)), full((d, LANES)), full((1, LANES)),
                  full((d, LANES)), full((d, LANES)), full((1, LANES))],
        out_specs=[row_spec, row_spec, lane_spec, lane_spec, full((1, LANES))],
        out_shape=[jax.ShapeDtypeStruct((m, d), F32), jax.ShapeDtypeStruct((m, d), F32),
                   jax.ShapeDtypeStruct((m, LANES), jnp.int32), jax.ShapeDtypeStruct((m, LANES), F32),
                   jax.ShapeDtypeStruct((1, LANES), F32)],
        scratch_shapes=[pltpu.VMEM((1, LANES), F32)],
        compiler_params=_params(("arbitrary",)),
        name="outproj_route",
    )(merged, w_out, x2, mod4, mod4, mod4, g_ffn.reshape(1, d), wgh, wgl, bg, wrh, wrl, br)


DMA_UNROLL = 8


def _row_copy(src_hbm, row, dst, slot, sem):
    return pltpu.make_async_copy(src_hbm.at[pl.ds(row, 1), :], dst.at[pl.ds(slot, 1), :], sem)


def _expert_kernel(bexp_ref, nb_ref, tok_ref, h2_hbm, wg_ref, wu_ref, wd_ref, o_ref, xbuf, sems):
    blk = pl.program_id(0)
    n_used = nb_ref[0]

    def gather(block, buf):
        base = block * MOE_BLOCK

        def issue(r, carry):
            _row_copy(h2_hbm, tok_ref[base + r], xbuf.at[buf], r, sems.at[buf]).start()
            return carry

        lax.fori_loop(0, MOE_BLOCK, issue, 0, unroll=DMA_UNROLL)

    @pl.when(blk == 0)
    def _():
        gather(0, 0)

    @pl.when(blk < n_used)
    def _():
        cur = blk % 2

        @pl.when(blk + 1 < n_used)
        def _():
            gather(blk + 1, 1 - cur)

        def drain(r, carry):
            _row_copy(h2_hbm, 0, xbuf.at[cur], r, sems.at[cur]).wait()
            return carry

        lax.fori_loop(0, MOE_BLOCK, drain, 0, unroll=DMA_UNROLL)
        x = xbuf[cur].astype(BF16)
        gate = _dot(x, wg_ref[0])
        up = _dot(x, wu_ref[0])
        act = (_silu(gate) * up).astype(BF16)
        o_ref[...] = _dot(act, wd_ref[0])

    @pl.when(blk >= n_used)
    def _():
        o_ref[...] = jnp.zeros_like(o_ref)


def _experts(h2, slot_tok, block_expert, n_used, w_gate_b, w_up_b, w_down_b):
    t, d = h2.shape
    n_slots = slot_tok.shape[0]
    n_blocks = n_slots // MOE_BLOCK
    hid = w_up_b.shape[2]
    grid_spec = pltpu.PrefetchScalarGridSpec(
        num_scalar_prefetch=3,
        grid=(n_blocks,),
        in_specs=[pl.BlockSpec(memory_space=pl.ANY),
                  pl.BlockSpec((1, d, hid), lambda i, be, nb, tok: (be[i], 0, 0)),
                  pl.BlockSpec((1, d, hid), lambda i, be, nb, tok: (be[i], 0, 0)),
                  pl.BlockSpec((1, hid, d), lambda i, be, nb, tok: (be[i], 0, 0))],
        out_specs=pl.BlockSpec((MOE_BLOCK, d), lambda i, be, nb, tok: (i, 0)),
        scratch_shapes=[pltpu.VMEM((2, MOE_BLOCK, d), F32), pltpu.SemaphoreType.DMA((2,))])
    return pl.pallas_call(
        _expert_kernel, grid_spec=grid_spec,
        out_shape=jax.ShapeDtypeStruct((n_slots, d), F32),
        compiler_params=_params(("arbitrary",)),
        name="experts",
    )(block_expert, n_used, slot_tok, h2, w_gate_b, w_up_b, w_down_b)


def _combine_kernel(dest_ref, yb_hbm, ew_ref, x1_ref, g2_ref, gfin_ref, o_ref, buf, sem):
    i = pl.program_id(0)
    tm = x1_ref.shape[0]
    n_rows = buf.shape[0]
    base = i * n_rows

    def issue(r, carry):
        _row_copy(yb_hbm, dest_ref[base + r], buf, r, sem).start()
        return carry

    lax.fori_loop(0, n_rows, issue, 0, unroll=DMA_UNROLL)

    def drain(r, carry):
        _row_copy(yb_hbm, 0, buf, r, sem).wait()
        return carry

    lax.fori_loop(0, n_rows, drain, 0, unroll=DMA_UNROLL)
    w1 = ew_ref[:, 0:1]
    w2 = ew_ref[:, 1:2]
    moe = buf[0:tm, :] * w1 + buf[tm:2 * tm, :] * w2
    x2 = x1_ref[...] + g2_ref[0, 0] * moe
    xn = x2 * lax.rsqrt(jnp.mean(x2 * x2, axis=-1, keepdims=True) + NORM_EPS)
    o_ref[...] = xn * gfin_ref[...]


def _combine(yb, dest_tiles, ew, x1, mod4, t_seq, g_final, tm):
    m, d = x1.shape
    r = mod4.shape[2]
    assert t_seq % tm == 0 and (r == 1 or tm == t_seq)
    per_seq = t_seq // tm
    grid_spec = pltpu.PrefetchScalarGridSpec(
        num_scalar_prefetch=1,
        grid=(m // tm,),
        in_specs=[pl.BlockSpec(memory_space=pl.ANY),
                  pl.BlockSpec((tm, LANES), lambda i, de: (i, 0)),
                  pl.BlockSpec((tm, d), lambda i, de: (i, 0)),
                  pl.BlockSpec((1, 1, r, d), lambda i, de: (5, i // per_seq, 0, 0)),
                  pl.BlockSpec((1, d), lambda i, de: (0, 0))],
        out_specs=pl.BlockSpec((tm, d), lambda i, de: (i, 0)),
        scratch_shapes=[pltpu.VMEM((2 * tm, d), F32), pltpu.SemaphoreType.DMA])
    return pl.pallas_call(
        _combine_kernel, grid_spec=grid_spec,
        out_shape=jax.ShapeDtypeStruct((m, d), F32),
        compiler_params=_params(("arbitrary",)),
        name="combine",
    )(dest_tiles, yb, ew, x1, mod4, g_final.reshape(1, d))


def _moe_and_final_norm(merged, x2, mod4, t_seq, lw, g_final, tm_route, tm_comb):
    m, d = x2.shape
    x1, h2, ids, ew, cnt = _outproj_route(merged, lw["w_out_b"], x2, mod4, t_seq, lw["g_ffn"],
                                          lw["w_group"], lw["b_group"], lw["w_router"], lw["b_router"],
                                          tm_route)
    e_id = ids[:, 0:2]
    rank = ids[:, 2:4]
    counts = cnt[0, :N_EXPERTS].astype(jnp.int32)
    padded = (counts + MOE_BLOCK - 1) // MOE_BLOCK * MOE_BLOCK
    pad_end = jnp.cumsum(padded)
    pad_start = pad_end - padded
    experts = jnp.arange(N_EXPERTS, dtype=jnp.int32)
    dest = jnp.sum(jnp.where(e_id[..., None] == experts, pad_start, 0), axis=-1) + rank
    n_assign = 2 * m
    n_blocks = (n_assign + N_EXPERTS * (MOE_BLOCK - 1) + MOE_BLOCK - 1) // MOE_BLOCK
    n_slots = n_blocks * MOE_BLOCK
    tok = jnp.broadcast_to(jnp.arange(m, dtype=jnp.int32)[:, None], (m, 2))
    slot_tok = (jnp.arange(n_slots, dtype=jnp.int32) % m).at[dest.reshape(-1)].set(tok.reshape(-1))
    block_start = jnp.arange(n_blocks, dtype=jnp.int32) * MOE_BLOCK
    block_expert = jnp.minimum(jnp.sum((pad_end[None, :] <= block_start[:, None]).astype(jnp.int32), axis=1),
                               N_EXPERTS - 1)
    n_used = (pad_end[-1:] // MOE_BLOCK).astype(jnp.int32)
    yb = _experts(h2, slot_tok, block_expert, n_used, lw["w_gate_e_b"], lw["w_up_b"], lw["w_down_b"])
    dest_tiles = dest.reshape(m // tm_comb, tm_comb, 2).transpose(0, 2, 1).reshape(-1).astype(jnp.int32)
    return _combine(yb, dest_tiles, ew, x1, mod4, t_seq, g_final, tm_comb)


def _input_projections(h1, lw, tm):
    proj = {name: _matmul_wcast(h1, lw["w_in"], lw["layer"], col0, n, tm, 1024)
            for name, (col0, n) in lw["main_cols"].items()}
    proj.update({name: _matmul(h1, lw["w_" + name], tm, 512) for name in ("ga", "gs")})
    proj["dt"] = _dt_proj(h1, lw["w_dt"], lw["dt_bias"], tm)
    return proj


def _dt_layouts(dt_raw, bm, t, t_pad):
    hpg = HEADS_PER_GROUP
    d = dt_raw[:, :SSM_GROUPS * hpg].reshape(bm, t, SSM_GROUPS, hpg)
    d = jnp.pad(d, ((0, 0), (0, t_pad - t), (0, 0), (0, 0)))
    dt_g = jnp.pad(d.transpose(0, 2, 1, 3), ((0, 0), (0, 0), (0, 0), (0, LANES - hpg)))
    dtt_g = d.transpose(0, 2, 3, 1)
    return dt_g, dtt_g


def kernel(x_prompt, x_sample, cache_k, cache_v, state_ssm, state_conv, page_table, c_prompt, c_sample,
           w_ada, b_ada, g_mix, w_in, b_gate, conv_w, conv_b, dt_bias, a_log, d_skip, g_ssm,
           w_attn_br, w_ssm_br, w_out, g_ffn, w_group, b_group, w_router, b_router,
           w_up, w_gate_e, w_down, g_final):
    depth = w_ada.shape[0]
    assert depth == 1
    bp, t, d = x_prompt.shape
    bd, tq, _ = x_sample.shape
    att_w = N_HEADS * (d // N_HEADS)
    d_inner = g_ssm.shape[1]
    conv_dim = conv_w.shape[2]
    n_ssm_heads = dt_bias.shape[1]
    assert n_ssm_heads == SSM_GROUPS * HEADS_PER_GROUP and d_inner == n_ssm_heads * SSM_HEADDIM
    slopes = 2.0 ** (-8.0 * jnp.arange(1, N_HEADS + 1, dtype=F32) / N_HEADS)

    l = 0
    tail0 = 3 * att_w + d_inner + conv_dim
    w_in_b = jnp.pad(w_in[l][:, tail0:].astype(BF16), ((0, 0), (tail0, 0)))
    cuts = [0, att_w, 2 * att_w, 3 * att_w, 3 * att_w + d_inner, 3 * att_w + d_inner + conv_dim,
            3 * att_w + d_inner + conv_dim + n_ssm_heads]
    cuts += [cuts[-1] + d, cuts[-1] + 2 * d]
    names = ("q", "k", "v", "z", "xbc", "dt", "ga", "gs")
    lw = {"w_" + n: w_in_b[:, cuts[i]:cuts[i + 1]] for i, n in enumerate(names)}
    lw["w_dt"] = jnp.pad(lw["w_dt"], ((0, 0), (0, LANES - n_ssm_heads)))
    lw["w_in"], lw["layer"] = w_in, l
    lw["main_cols"] = {n: (cuts[i], cuts[i + 1] - cuts[i]) for i, n in enumerate(names[:5])}
    lw["dt_bias"] = jnp.pad(dt_bias[l], (0, LANES - n_ssm_heads)).reshape(1, LANES)
    lw.update(w_attn_b=w_attn_br[l].astype(BF16), w_ssm_b=w_ssm_br[l].astype(BF16),
              w_out_b=w_out[l].astype(BF16), g_ffn=g_ffn[l], w_group=w_group[l], b_group=b_group[l],
              w_router=w_router[l], b_router=b_router[l], w_up_b=w_up[l].astype(BF16),
              w_gate_e_b=w_gate_e[l].astype(BF16), w_down_b=w_down[l].astype(BF16))

    mod = _ada_mod(jnp.concatenate([c_prompt, c_sample], axis=0), w_ada[l], b_ada[l])
    mod = mod.reshape(bp + bd, 6, d)
    mod_p = mod[:bp].transpose(1, 0, 2).reshape(6, bp, 1, d)
    mod_s = jnp.repeat(mod[bp:].transpose(1, 0, 2), tq, axis=1).reshape(6, 1, bd * tq, d)

    ssd_args = (conv_w[l], conv_b[l], a_log[l], d_skip[l], g_ssm[l])

    mp = bp * t
    h1 = _modnorm(x_prompt, g_mix[l], mod_p, 1, 0, 512)
    pj = _input_projections(h1, lw, 1024)
    q3, k3, v3 = (pj[n].reshape(bp, t, att_w) for n in ("q", "k", "v"))
    attn = _moba_prompt(q3, k3, v3, slopes).reshape(mp, att_w)
    xbc3 = pj["xbc"].reshape(bp, t, conv_dim)
    dt_g, dtt_g = _dt_layouts(pj["dt"], bp, t, t)
    y, ssm_p = _ssd(xbc3, pj["z"].reshape(bp, t, d_inner), dt_g, dtt_g,
                    jnp.zeros((bp, 3, conv_dim), F32), jnp.zeros((bp, d_inner, SSM_STATE), F32),
                    *ssd_args)
    merged = _merge(attn, y.reshape(mp, d_inner), lw["w_attn_b"], lw["w_ssm_b"], pj["ga"], pj["gs"],
                    b_gate[l], 512, 512)
    y_prompt = _moe_and_final_norm(merged, x_prompt.reshape(mp, d), mod_p, t, lw, g_final, 256, 128)
    y_prompt = y_prompt.reshape(bp, t, d)
    k_p = k3.reshape(1, bp, t, N_HEADS, att_w // N_HEADS)
    v_p = v3.reshape(1, bp, t, N_HEADS, att_w // N_HEADS)
    conv_p = xbc3[:, t - 3:, :][None]
    ssm_p = ssm_p.reshape(1, bp, n_ssm_heads, SSM_HEADDIM, SSM_STATE)

    ms = bd * tq
    xs3 = x_sample.reshape(1, ms, d)
    h1s = _modnorm(xs3, g_mix[l], mod_s, 1, 0, ms)
    pjs = _input_projections(h1s, lw, ms)
    qs, ks, vs = (pjs[n].reshape(bd, tq, att_w) for n in ("q", "k", "v"))
    attn_s = _moba_sample(qs, ks, vs, cache_k, cache_v, l, page_table, slopes)
    attn_s = attn_s.reshape(ms, att_w).astype(BF16)
    xbc_s = pjs["xbc"].reshape(bd, tq, conv_dim)
    pad_t = lambda a: jnp.pad(a, ((0, 0), (0, SSD_CHUNK - tq), (0, 0)))
    dt_gs, dtt_gs = _dt_layouts(pjs["dt"], bd, tq, SSD_CHUNK)
    y_s, ssm_s = _ssd(pad_t(xbc_s), pad_t(pjs["z"].reshape(bd, tq, d_inner)), dt_gs, dtt_gs,
                      state_conv[l], state_ssm[l].reshape(bd, d_inner, SSM_STATE), *ssd_args)
    y_s = y_s[:, :tq, :].reshape(ms, d_inner)
    merged_s = _merge(attn_s, y_s, lw["w_attn_b"], lw["w_ssm_b"], pjs["ga"], pjs["gs"], b_gate[l], ms, 512)
    y_sample = _moe_and_final_norm(merged_s, x_sample.reshape(ms, d), mod_s, ms, lw, g_final, ms, ms)
    y_sample = y_sample.reshape(bd, tq, d)
    k_s = ks.reshape(1, bd, tq, N_HEADS, att_w // N_HEADS)
    v_s = vs.reshape(1, bd, tq, N_HEADS, att_w // N_HEADS)
    conv_s = jnp.concatenate([state_conv[l], xbc_s], axis=1)[:, tq:, :][None]
    ssm_s = ssm_s.reshape(1, bd, n_ssm_heads, SSM_HEADDIM, SSM_STATE)

    return (y_prompt, y_sample, k_p, v_p, k_s, v_s, ssm_p, ssm_s, conv_p, conv_s)
```
